```python
import functools
import jax
import jax.numpy as jnp
from jax import lax
import numpy as np

D_MODEL = 2048
BATCH = 32
SEQ = 256
DEPTH = 4
DEC_BATCH = 4
DEC_SEQ = 1024
PAST_LEN = 512

GRID_W = 64
D_ATTN = D_MODEL // 2
N_HEADS = 16
HEAD_DIM = D_ATTN // N_HEADS
WIN_H = 8
WIN_W = 16
D_CONV = D_MODEL // 4
CONV_WIDTH = 31
D_POOL = D_MODEL // 4
POOL_WINDOWS = (2, 4, 8, 16)
POOL_GROUP = D_POOL // len(POOL_WINDOWS)
D_FF = ((8 * D_MODEL // 3 + 127) // 128) * 128
N_MOD = 9
Q_BLOCK = 128
EPS = 1e-6
IN_WIDTH = 3 * D_ATTN + 2 * D_CONV + D_POOL + 3 * D_MODEL
IN_SPLITS = (D_ATTN, 2 * D_ATTN, 3 * D_ATTN, 3 * D_ATTN + 2 * D_CONV, 3 * D_ATTN + 2 * D_CONV + D_POOL)

kernel_name = 'hybrid_na_conv_pool_flow_step'


def rmsnorm(x, g):
    xf = x.astype(jnp.float32)
    y = xf * lax.rsqrt(jnp.mean(xf * xf, axis=-1, keepdims=True) + EPS)
    return (y * g.astype(jnp.float32)).astype(x.dtype)


def swiglu(h, w1, w3, w2):
    return (jax.nn.silu(h @ w1) * (h @ w3)) @ w2


def context_attention(q, k, v):
    B, S, H, Dh = q.shape
    nb = S // Q_BLOCK
    qb = (q * (HEAD_DIM ** -0.5)).reshape(B, nb, Q_BLOCK, H, Dh).transpose(1, 0, 2, 3, 4)

    def block(qi):
        s = jnp.einsum('bqhd,bkhd->bhqk', qi, k).astype(jnp.float32)
        pr = jax.nn.softmax(s, axis=-1).astype(v.dtype)
        return jnp.einsum('bhqk,bkhd->bqhd', pr, v)

    out = lax.map(block, qb)
    return out.transpose(1, 0, 2, 3, 4).reshape(B, S, H, Dh)


def latent_attention(q, k, v, ck, cv, rpb):
    B, N, H, Dh = q.shape
    rows = N // GRID_W
    kh = min(WIN_H, rows)
    nb = rows // 2
    r = jnp.arange(rows)
    row_idx = jnp.clip(r - kh // 2, 0, rows - kh)[:, None] + jnp.arange(kh)[None, :]
    dr = row_idx - r[:, None]
    col = jnp.arange(GRID_W)
    col_start = jnp.clip(col - WIN_W // 2, 0, GRID_W - WIN_W)
    col_in = (col[None, :] >= col_start[:, None]) & (col[None, :] < col_start[:, None] + WIN_W)
    dc_idx = jnp.clip(col[None, :] - col[:, None], -(WIN_W - 1), WIN_W - 1) + (WIN_W - 1)
    kg = k.reshape(B, rows, GRID_W, H, Dh)
    vg = v.reshape(B, rows, GRID_W, H, Dh)
    qb_all = (q * (HEAD_DIM ** -0.5)).reshape(B, nb, 2, GRID_W, H, Dh).transpose(1, 0, 2, 3, 4, 5)

    def block(args):
        qb, bi = args
        rb = 2 * bi + jnp.arange(2)
        idx = row_idx[rb]
        kb = kg[:, idx]
        vb = vg[:, idx]
        bias = rpb[:, dr[rb] + (WIN_H - 1)][..., dc_idx]
        bias = bias.transpose(0, 1, 3, 2, 4).astype(jnp.float32)
        s_loc = jnp.einsum('bpqhd,bpkwhd->bhpqkw', qb, kb).astype(jnp.float32) + bias
        s_loc = jnp.where(col_in[:, None, :], s_loc, -jnp.inf).reshape(B, H, 2, GRID_W, kh * GRID_W)
        s_ctx = jnp.einsum('bpqhd,bkhd->bhpqk', qb, ck).astype(jnp.float32)
        pr = jax.nn.softmax(jnp.concatenate([s_loc, s_ctx], axis=-1), axis=-1).astype(v.dtype)
        p_loc = pr[..., :kh * GRID_W].reshape(B, H, 2, GRID_W, kh, GRID_W)
        p_ctx = pr[..., kh * GRID_W:]
        return (jnp.einsum('bhpqkw,bpkwhd->bpqhd', p_loc, vb)
                + jnp.einsum('bhpqk,bkhd->bpqhd', p_ctx, cv))

    out = lax.map(block, (qb_all, jnp.arange(nb)))
    return out.transpose(1, 0, 2, 3, 4, 5).reshape(B, N, H, Dh)


def conv_module(z, dw, b, g):
    a, gt = jnp.split(z, 2, axis=-1)
    h = a * jax.nn.sigmoid(gt)
    h = lax.conv_general_dilated(h, dw[:, None, :], window_strides=(1,),
                                 padding=[(CONV_WIDTH // 2, CONV_WIDTH // 2)],
                                 dimension_numbers=('NWC', 'WIO', 'NWC'),
                                 feature_group_count=D_CONV) + b
    return jax.nn.silu(rmsnorm(h, g))


def pool_mixer(z, pool_w, pool_scale):
    B, N, C = z.shape
    zf = z.astype(jnp.float32)
    cs = jnp.concatenate([jnp.zeros((B, 1, C), jnp.float32), jnp.cumsum(zf, axis=1)], axis=1)
    t = jnp.arange(N)
    outs = []
    for gi, w in enumerate(POOL_WINDOWS):
        lo = jnp.clip(t - w // 2, 0, N)
        hi = jnp.clip(t - w // 2 + w, 0, N)
        sl = slice(gi * POOL_GROUP, (gi + 1) * POOL_GROUP)
        mean = (cs[:, hi, sl] - cs[:, lo, sl]) / (hi - lo).astype(jnp.float32)[None, :, None]
        d = (mean - zf[:, :, sl]).astype(z.dtype)
        outs.append(d @ pool_w[gi])
    return jnp.concatenate(outs, axis=-1) * pool_scale


def trunk_layer(x, mod, attend, norm_g, w1, w3, w2, w_in, qk_g, w_br_a, conv_dw, conv_b, conv_g,
                w_br_b, pool_w, pool_scale, w_br_c, w_out):
    sh1, sc1, g1, sh2, sc2, g2, sh3, sc3, g3 = jnp.split(mod[:, None, :], N_MOD, axis=-1)
    B, N, _ = x.shape
    h = rmsnorm(x, norm_g[0]) * (1.0 + sc1) + sh1
    x = x + 0.5 * g1 * swiglu(h, w1[0], w3[0], w2[0])
    u = rmsnorm(x, norm_g[1]) * (1.0 + sc2) + sh2
    q, k, v, conv_in, pool_in, gates = jnp.split(u @ w_in, IN_SPLITS, axis=-1)
    q = rmsnorm(q.reshape(B, N, N_HEADS, HEAD_DIM), qk_g[0])
    k = rmsnorm(k.reshape(B, N, N_HEADS, HEAD_DIM), qk_g[1])
    v = v.reshape(B, N, N_HEADS, HEAD_DIM)
    br_a = attend(q, k, v).reshape(B, N, D_ATTN) @ w_br_a
    br_b = conv_module(conv_in, conv_dw, conv_b, conv_g) @ w_br_b
    br_c = pool_mixer(pool_in, pool_w, pool_scale) @ w_br_c
    ga, gb, gc = jnp.split(jax.nn.sigmoid(gates), 3, axis=-1)
    x = x + g2 * ((ga * br_a + gb * br_b + gc * br_c) @ w_out)
    h = rmsnorm(x, norm_g[2]) * (1.0 + sc3) + sh3
    x = x + 0.5 * g3 * swiglu(h, w1[1], w3[1], w2[1])
    return x, k, v


def setup_inputs(seed: int = 0) -> dict:
    key = jax.random.key(seed)
    ks = jax.random.split(key, 24)

    def nrm(k, shape, scale):
        return jax.random.normal(k, shape, jnp.float32) * scale

    return {
        'x_prompt': nrm(ks[0], (BATCH, SEQ, D_MODEL), 1.0),
        'x_sample': nrm(ks[1], (DEC_BATCH, DEC_SEQ, D_MODEL), 1.0),
        'cache_k': nrm(ks[2], (DEC_BATCH, DEPTH, PAST_LEN, N_HEADS, HEAD_DIM), 1.0),
        'cache_v': nrm(ks[3], (DEC_BATCH, DEPTH, PAST_LEN, N_HEADS, HEAD_DIM), 1.0),
        'c': nrm(ks[4], (DEC_BATCH, D_MODEL), 1.0),
        'c_ctx': nrm(ks[5], (D_MODEL,), 1.0),
        'w_mod': nrm(ks[6], (DEPTH, D_MODEL, N_MOD * D_MODEL), D_MODEL ** -0.5),
        'b_mod': nrm(ks[7], (DEPTH, N_MOD * D_MODEL), 0.01),
        'norm_g': 1.0 + nrm(ks[8], (DEPTH, 3, D_MODEL), 0.02),
        'ffn_w1': nrm(ks[9], (DEPTH, 2, D_MODEL, D_FF), D_MODEL ** -0.5),
        'ffn_w3': nrm(ks[10], (DEPTH, 2, D_MODEL, D_FF), D_MODEL ** -0.5),
        'ffn_w2': nrm(ks[11], (DEPTH, 2, D_FF, D_MODEL), D_FF ** -0.5),
        'w_in': nrm(ks[12], (DEPTH, D_MODEL, IN_WIDTH), D_MODEL ** -0.5),
        'qk_g': 1.0 + nrm(ks[13], (DEPTH, 2, HEAD_DIM), 0.02),
        'rpb': nrm(ks[14], (DEPTH, N_HEADS, 2 * WIN_H - 1, 2 * WIN_W - 1), 0.1),
        'w_br_a': nrm(ks[15], (DEPTH, D_ATTN, D_MODEL), D_ATTN ** -0.5),
        'conv_dw': nrm(ks[16], (DEPTH, CONV_WIDTH, D_CONV), CONV_WIDTH ** -0.5),
        'conv_b': nrm(ks[17], (DEPTH, D_CONV), 0.01),
        'conv_g': 1.0 + nrm(ks[18], (DEPTH, D_CONV), 0.02),
        'w_br_b': nrm(ks[19], (DEPTH, D_CONV, D_MODEL), D_CONV ** -0.5),
        'pool_w': nrm(ks[20], (DEPTH, len(POOL_WINDOWS), POOL_GROUP, POOL_GROUP), POOL_GROUP ** -0.5),
        'pool_scale': 1.0 + nrm(ks[21], (DEPTH, D_POOL), 0.02),
        'w_br_c': nrm(ks[22], (DEPTH, D_POOL, D_MODEL), D_POOL ** -0.5),
        'w_out': nrm(ks[23], (DEPTH, D_MODEL, D_MODEL), D_MODEL ** -0.5),
    }


def reference(x_prompt, x_sample, cache_k, cache_v, c, c_ctx, w_mod, b_mod, norm_g, ffn_w1, ffn_w3,
              ffn_w2, w_in, qk_g, rpb, w_br_a, conv_dw, conv_b, conv_g, w_br_b, pool_w, pool_scale,
              w_br_c, w_out):
    xp = x_prompt
    xs = x_sample
    new_ks = []
    new_vs = []
    for l in range(DEPTH):
        shared = (norm_g[l], ffn_w1[l], ffn_w3[l], ffn_w2[l], w_in[l], qk_g[l], w_br_a[l], conv_dw[l],
                  conv_b[l], conv_g[l], w_br_b[l], pool_w[l], pool_scale[l], w_br_c[l], w_out[l])
        mod_ctx = jax.nn.silu(c_ctx)[None, :] @ w_mod[l] + b_mod[l]
        mod_lat = jax.nn.silu(c) @ w_mod[l] + b_mod[l]
        xp, k_l, v_l = trunk_layer(xp, mod_ctx, context_attention, *shared)
        new_ks.append(k_l)
        new_vs.append(v_l)
        attend = functools.partial(latent_attention, ck=cache_k[:, l], cv=cache_v[:, l], rpb=rpb[l])
        xs, _, _ = trunk_layer(xs, mod_lat, attend, *shared)
    new_k = jnp.stack(new_ks, axis=1)
    new_v = jnp.stack(new_vs, axis=1)
    return (xp, xs, new_k, new_v)
```

```python
import functools

import jax
import jax.numpy as jnp
from jax import lax
from jax.experimental import pallas as pl
from jax.experimental.pallas import tpu as pltpu

D_MODEL = 2048
BATCH = 32
SEQ = 256
DEPTH = 4
DEC_BATCH = 4
DEC_SEQ = 1024
PAST_LEN = 512
GRID_W = 64
GRID_ROWS = DEC_SEQ // GRID_W
D_ATTN = D_MODEL // 2
N_HEADS = 16
HEAD_DIM = D_ATTN // N_HEADS
WIN_H = 8
WIN_W = 16
D_CONV = D_MODEL // 4
CONV_WIDTH = 31
D_POOL = D_MODEL // 4
POOL_WINDOWS = (2, 4, 8, 16)
POOL_GROUP = D_POOL // len(POOL_WINDOWS)
D_FF = ((8 * D_MODEL // 3 + 127) // 128) * 128
N_MOD = 9
EPS = 1e-6
IN_WIDTH = 3 * D_ATTN + 2 * D_CONV + D_POOL + 3 * D_MODEL

N_CTX = BATCH * SEQ
N_LAT = DEC_BATCH * DEC_SEQ
N_TOK = N_CTX + N_LAT
GROUP = 256
N_GROUPS = N_TOK // GROUP
CTX_GROUPS = N_CTX // GROUP
LAT_GROUPS_PER_SEQ = DEC_SEQ // GROUP

LANES = 128
HEAD_PAIRS = N_HEADS // 2
HALO = 16

COL_Q = 0
COL_K = D_ATTN
COL_V = 2 * D_ATTN
COL_CONV = 3 * D_ATTN
COL_GATE = COL_CONV + 2 * D_CONV
COL_POOL = COL_GATE + 3 * D_MODEL

FF_TILE = 512
D_FF_PAD = ((D_FF + FF_TILE - 1) // FF_TILE) * FF_TILE
N_FF_TILES = D_FF_PAD // FF_TILE
TOK_TILE = 512
IN_TILE = 1536
MERGE_TILE = 256
MOD_TILE = 1024

Q_ROWS = 4
Q_BLK = Q_ROWS * GRID_W
N_QBLK = GRID_ROWS // Q_ROWS
KEY_ROWS = 12
KEY_BLK = KEY_ROWS * GRID_W
NEG = -1e30

VMEM_LIMIT = 48 * 1024 * 1024

F32 = jnp.float32
BF16 = jnp.bfloat16


def _params(sem, vmem=VMEM_LIMIT):
    return pltpu.CompilerParams(dimension_semantics=sem, vmem_limit_bytes=vmem)


def _dot(a, b):
    return jnp.dot(a, b, preferred_element_type=F32)


def _dot_nt(a, b):
    return lax.dot_general(a, b, (((1,), (1,)), ((), ())), preferred_element_type=F32)


def _adaln(x, g, sc, sh):
    ms = jnp.mean(x * x, axis=-1, keepdims=True)
    y = (x * lax.rsqrt(ms + EPS)) * g
    return y * (1.0 + sc) + sh


def _mod_kernel(c_ref, w_ref, b_ref, o_ref):
    c = c_ref[...]
    s = (c * jax.nn.sigmoid(c)).astype(BF16)
    o_ref[...] = _dot(s, w_ref[...].astype(BF16)) + b_ref[...]


def _modulation(cpad, w_mod, b_mod):
    width = N_MOD * D_MODEL
    return pl.pallas_call(
        _mod_kernel,
        grid=(DEPTH, width // MOD_TILE),
        in_specs=[
            pl.BlockSpec((8, D_MODEL), lambda l, n: (0, 0)),
            pl.BlockSpec((None, D_MODEL, MOD_TILE), lambda l, n: (l, 0, n)),
            pl.BlockSpec((None, 1, MOD_TILE), lambda l, n: (l, 0, n)),
        ],
        out_specs=pl.BlockSpec((None, 8, MOD_TILE), lambda l, n: (l, 0, n)),
        out_shape=jax.ShapeDtypeStruct((DEPTH, 8, width), F32),
        compiler_params=_params(("arbitrary", "arbitrary")),
        name="modulation",
    )(cpad, w_mod, b_mod.reshape(DEPTH, 1, width))


def _ffn_kernel(x_ref, ng_ref, sh_ref, sc_ref, gt_ref, w1_ref, w3_ref, w2_ref, o_ref, h_ref):
    @pl.when(pl.program_id(1) == 0)
    def _():
        x = x_ref[...]
        h_ref[...] = _adaln(x, ng_ref[...], sc_ref[...], sh_ref[...]).astype(BF16)
        o_ref[...] = x

    h = h_ref[...]
    a = _dot(h, w1_ref[...])
    b = _dot(h, w3_ref[...])
    g = (a * jax.nn.sigmoid(a) * b).astype(BF16)
    o_ref[...] += (0.5 * gt_ref[...]) * _dot(g, w2_ref[...])


def _ffn(x, ng, sh, sc, gt, w1t, w3t, w2t, l, s):
    step = TOK_TILE // GROUP
    mod_spec = pl.BlockSpec((None, 1, D_MODEL), lambda i, j: (i * step, 0, 0))
    return pl.pallas_call(
        _ffn_kernel,
        grid=(N_TOK // TOK_TILE, N_FF_TILES),
        in_specs=[
            pl.BlockSpec((TOK_TILE, D_MODEL), lambda i, j: (i, 0)),
            pl.BlockSpec((1, D_MODEL), lambda i, j: (0, 0)),
            mod_spec, mod_spec, mod_spec,
            pl.BlockSpec((None, None, None, D_MODEL, FF_TILE), lambda i, j: (l, s, j, 0, 0)),
            pl.BlockSpec((None, None, None, D_MODEL, FF_TILE), lambda i, j: (l, s, j, 0, 0)),
            pl.BlockSpec((None, None, None, FF_TILE, D_MODEL), lambda i, j: (l, s, j, 0, 0)),
        ],
        out_specs=pl.BlockSpec((TOK_TILE, D_MODEL), lambda i, j: (i, 0)),
        out_shape=jax.ShapeDtypeStruct((N_TOK, D_MODEL), F32),
        scratch_shapes=[pltpu.VMEM((TOK_TILE, D_MODEL), BF16)],
        compiler_params=_params(("arbitrary", "arbitrary")),
        name="ffn",
    )(x, ng, sh, sc, gt, w1t, w3t, w2t)


def _inproj_kernel(x_ref, ng_ref, sh_ref, sc_ref, w_ref, o_ref, h_ref):
    @pl.when(pl.program_id(1) == 0)
    def _():
        h_ref[...] = _adaln(x_ref[...], ng_ref[...], sc_ref[...], sh_ref[...]).astype(BF16)

    o_ref[...] = _dot(h_ref[...], w_ref[...])


def _inproj(x, ng, sh, sc, w_in, l):
    step = TOK_TILE // GROUP
    mod_spec = pl.BlockSpec((None, 1, D_MODEL), lambda i, j: (i * step, 0, 0))
    return pl.pallas_call(
        _inproj_kernel,
        grid=(N_TOK // TOK_TILE, IN_WIDTH // IN_TILE),
        in_specs=[
            pl.BlockSpec((TOK_TILE, D_MODEL), lambda i, j: (i, 0)),
            pl.BlockSpec((1, D_MODEL), lambda i, j: (0, 0)),
            mod_spec, mod_spec,
            pl.BlockSpec((None, D_MODEL, IN_TILE), lambda i, j: (l, 0, j)),
        ],
        out_specs=pl.BlockSpec((TOK_TILE, IN_TILE), lambda i, j: (i, j)),
        out_shape=jax.ShapeDtypeStruct((N_TOK, IN_WIDTH), F32),
        scratch_shapes=[pltpu.VMEM((TOK_TILE, D_MODEL), BF16)],
        compiler_params=_params(("arbitrary", "arbitrary")),
        name="inproj",
    )(x, ng, sh, sc, w_in)


def _pair_rmsnorm(x, g, first):
    x2 = x * x
    s0 = jnp.sum(jnp.where(first, x2, 0.0), axis=-1, keepdims=True)
    s1 = jnp.sum(jnp.where(first, 0.0, x2), axis=-1, keepdims=True)
    inv = 1.0 / HEAD_DIM
    r = jnp.where(first, lax.rsqrt(s0 * inv + EPS), lax.rsqrt(s1 * inv + EPS))
    return (x * r) * g


def _first_head_mask():
    return lax.broadcasted_iota(jnp.int32, (1, LANES), 1) < HEAD_DIM


def _ctx_attn_kernel(q_ref, k_ref, v_ref, qg_ref, kg_ref, nk_in, nv_in, o_ref, nk_ref, nv_ref):
    del nk_in, nv_in
    first = _first_head_mask()
    for p in range(HEAD_PAIRS):
        cols = slice(p * LANES, (p + 1) * LANES)
        qn = _pair_rmsnorm(q_ref[:, cols], qg_ref[...], first) * (HEAD_DIM ** -0.5)
        kn = _pair_rmsnorm(k_ref[:, cols], kg_ref[...], first)
        v = v_ref[:, cols]
        nk_ref[:, cols] = kn
        nv_ref[:, cols] = v
        kb = kn.astype(BF16)
        vb = v.astype(BF16)
        outs = []
        for hh in range(2):
            sel = first if hh == 0 else jnp.logical_not(first)
            qm = jnp.where(sel, qn, 0.0).astype(BF16)
            s = _dot_nt(qm, kb)
            m = jnp.max(s, axis=-1, keepdims=True)
            e = jnp.exp(s - m)
            den = jnp.sum(e, axis=-1, keepdims=True)
            outs.append(_dot(e.astype(BF16), vb) / den)
        o_ref[:, cols] = jnp.where(first, outs[0], outs[1]).astype(BF16)


def _ctx_attention(u, qg, kg, new_k, new_v, l):
    qkv = lambda c: pl.BlockSpec((SEQ, D_ATTN), lambda b: (b, c))
    cache_spec = pl.BlockSpec((None, None, SEQ, D_ATTN), lambda b: (b, l, 0, 0))
    gain_spec = pl.BlockSpec((1, LANES), lambda b: (0, 0))
    any_spec = pl.BlockSpec(memory_space=pl.ANY)
    return pl.pallas_call(
        _ctx_attn_kernel,
        grid=(BATCH,),
        in_specs=[qkv(COL_Q // D_ATTN), qkv(COL_K // D_ATTN), qkv(COL_V // D_ATTN),
                  gain_spec, gain_spec, any_spec, any_spec],
        out_specs=[pl.BlockSpec((SEQ, D_ATTN), lambda b: (b, 0)), cache_spec, cache_spec],
        out_shape=[jax.ShapeDtypeStruct((N_TOK, D_ATTN), BF16),
                   jax.ShapeDtypeStruct(new_k.shape, F32),
                   jax.ShapeDtypeStruct(new_v.shape, F32)],
        input_output_aliases={5: 1, 6: 2},
        compiler_params=_params(("arbitrary",)),
        name="ctx_attention",
    )(u, u, u, qg, kg, new_k, new_v)


def _window_start(r):
    return min(max(r - WIN_H // 2, 0), GRID_ROWS - WIN_H)


def _key_window_row(blk):
    return min(max(_window_start(blk * Q_ROWS), 0), GRID_ROWS - KEY_ROWS)


def _bias_kernel(rpb_ref, o_ref, t_ref):
    h = pl.program_id(0)
    n_dc = 2 * WIN_W - 1
    n_dr = 2 * WIN_H - 1
    lane = lax.broadcasted_iota(jnp.int32, (GRID_W, LANES), 1)
    qc = lax.broadcasted_iota(jnp.int32, (GRID_W, LANES), 0)
    kc = lane & (GRID_W - 1)
    dc = kc - qc
    col_start = jnp.clip(qc - WIN_W // 2, 0, GRID_W - WIN_W)
    col_in = (kc >= col_start) & (kc < col_start + WIN_W)
    base = h * (n_dr * n_dc)
    for dr in range(n_dr):
        row = base + dr * n_dc
        t = jnp.full((GRID_W, LANES), rpb_ref[row], F32)
        for j in range(1, n_dc - 1):
            t = jnp.where(dc == j - (WIN_W - 1), rpb_ref[row + j], t)
        t = jnp.where(dc >= WIN_W - 1, rpb_ref[row + n_dc - 1], t)
        t_ref[dr] = jnp.where(col_in, t, NEG)

    left = lane < GRID_W
    neg = jnp.full((GRID_W, LANES), NEG, F32)
    for blk in range(N_QBLK):
        ws = _key_window_row(blk)
        for qi in range(Q_ROWS):
            qr = blk * Q_ROWS + qi
            st = _window_start(qr)

            def tile(kj):
                kr = ws + kj
                if st <= kr < st + WIN_H:
                    return t_ref[kr - qr + WIN_H - 1]
                return None

            for g in range(KEY_ROWS // 2):
                a, b = tile(2 * g), tile(2 * g + 1)
                if a is None and b is None:
                    val = neg
                else:
                    val = jnp.where(left, neg if a is None else a, neg if b is None else b)
                o_ref[blk, qi * GRID_W:(qi + 1) * GRID_W, g * LANES:(g + 1) * LANES] = val


def _bias_table(rpb_l):
    n = (2 * WIN_H - 1) * (2 * WIN_W - 1)
    return pl.pallas_call(
        _bias_kernel,
        grid=(N_HEADS,),
        in_specs=[pl.BlockSpec(memory_space=pltpu.SMEM)],
        out_specs=pl.BlockSpec((None, N_QBLK, Q_BLK, KEY_BLK), lambda h: (h, 0, 0, 0)),
        out_shape=jax.ShapeDtypeStruct((N_HEADS, N_QBLK, Q_BLK, KEY_BLK), F32),
        scratch_shapes=[pltpu.VMEM((2 * WIN_H - 1, GRID_W, LANES), F32)],
        compiler_params=_params(("arbitrary",)),
        name="bias_table",
    )(rpb_l.reshape(N_HEADS * n))


def _lat_attn_kernel(q_ref, k_ref, v_ref, ck_ref, cv_ref, bias_ref, qg_ref, kg_ref, o_in, o_ref):
    del o_in
    blk = pl.program_id(2)
    first = _first_head_mask()
    lo_row = _key_window_row(0) * GRID_W
    hi_row = _key_window_row(N_QBLK - 1) * GRID_W
    ws = pl.multiple_of(jnp.where(blk < N_QBLK // 2, lo_row, hi_row), GRID_W)
    qn = _pair_rmsnorm(q_ref[...], qg_ref[...], first) * (HEAD_DIM ** -0.5)
    kb = _pair_rmsnorm(k_ref[pl.ds(ws, KEY_BLK), :], kg_ref[...], first).astype(BF16)
    vb = v_ref[pl.ds(ws, KEY_BLK), :].astype(BF16)
    ckb = ck_ref[...].astype(BF16)
    cvb = cv_ref[...].astype(BF16)
    outs = []
    for hh in range(2):
        sel = first if hh == 0 else jnp.logical_not(first)
        qm = jnp.where(sel, qn, 0.0).astype(BF16)
        s_loc = _dot_nt(qm, kb) + bias_ref[hh]
        s_ctx = _dot_nt(qm, ckb)
        m = jnp.maximum(jnp.max(s_loc, axis=-1, keepdims=True), jnp.max(s_ctx, axis=-1, keepdims=True))
        e_loc = jnp.exp(s_loc - m)
        e_ctx = jnp.exp(s_ctx - m)
        den = jnp.sum(e_loc, axis=-1, keepdims=True) + jnp.sum(e_ctx, axis=-1, keepdims=True)
        acc = _dot(e_loc.astype(BF16), vb) + _dot(e_ctx.astype(BF16), cvb)
        outs.append(acc / den)
    o_ref[...] = jnp.where(first, outs[0], outs[1]).astype(BF16)


def _lat_attention(u, cache_k, cache_v, bias, qg, kg, attn, l):
    q_row0 = N_CTX // Q_BLK
    seq0 = N_CTX // DEC_SEQ
    gain_spec = pl.BlockSpec((1, LANES), lambda b, p, r: (0, 0))
    cache_spec = pl.BlockSpec((None, None, PAST_LEN, LANES), lambda b, p, r: (b, l, 0, p))
    return pl.pallas_call(
        _lat_attn_kernel,
        grid=(DEC_BATCH, HEAD_PAIRS, N_QBLK),
        in_specs=[
            pl.BlockSpec((Q_BLK, LANES), lambda b, p, r: (q_row0 + b * N_QBLK + r, COL_Q // LANES + p)),
            pl.BlockSpec((DEC_SEQ, LANES), lambda b, p, r: (seq0 + b, COL_K // LANES + p)),
            pl.BlockSpec((DEC_SEQ, LANES), lambda b, p, r: (seq0 + b, COL_V // LANES + p)),
            cache_spec, cache_spec,
            pl.BlockSpec((2, None, Q_BLK, KEY_BLK), lambda b, p, r: (p, r, 0, 0)),
            gain_spec, gain_spec,
            pl.BlockSpec(memory_space=pl.ANY),
        ],
        out_specs=pl.BlockSpec((Q_BLK, LANES), lambda b, p, r: (q_row0 + b * N_QBLK + r, p)),
        out_shape=jax.ShapeDtypeStruct((N_TOK, D_ATTN), BF16),
        input_output_aliases={8: 0},
        compiler_params=_params(("arbitrary", "arbitrary", "arbitrary")),
        name="lat_attention",
    )(u, u, u, cache_k, cache_v, bias, qg, kg, attn)


def _halo_masks(i):
    is_lat = i >= CTX_GROUPS
    j = i & (LAT_GROUPS_PER_SEQ - 1)
    return is_lat & (j != 0), is_lat & (j != LAT_GROUPS_PER_SEQ - 1)


def _conv_kernel(cur_ref, prev_ref, next_ref, dw_ref, cb_ref, cg_ref, o_ref, hp_ref, acc_ref):
    has_prev, has_next = _halo_masks(pl.program_id(0))

    def glu(z):
        return z[:, :D_CONV] * jax.nn.sigmoid(z[:, D_CONV:])

    hp_ref[0:HALO] = jnp.where(has_prev, glu(prev_ref[...]), 0.0)
    hp_ref[HALO:HALO + GROUP] = glu(cur_ref[...])
    hp_ref[HALO + GROUP:] = jnp.where(has_next, glu(next_ref[...]), 0.0)

    rows = 64
    off = HALO - CONV_WIDTH // 2
    for lc in range(D_CONV // LANES):
        cols = slice(lc * LANES, (lc + 1) * LANES)
        for rc in range(GROUP // rows):
            acc = jnp.zeros((rows, LANES), F32)
            for j in range(CONV_WIDTH):
                acc = acc + hp_ref[pl.ds(rc * rows + off + j, rows), cols] * dw_ref[j:j + 1, cols]
            acc_ref[rc * rows:(rc + 1) * rows, cols] = acc + cb_ref[:, cols]

    y = acc_ref[...]
    ms = jnp.mean(y * y, axis=-1, keepdims=True)
    y = (y * lax.rsqrt(ms + EPS)) * cg_ref[...]
    o_ref[...] = (y * jax.nn.sigmoid(y)).astype(BF16)


def _halo_specs(width, col):
    blocks_per_group = GROUP // HALO
    last = N_TOK // HALO - 1
    cur = pl.BlockSpec((GROUP, width), lambda i: (i, col))
    prev = pl.BlockSpec((HALO, width), lambda i: (jnp.maximum(i * blocks_per_group - 1, 0), col))
    nxt = pl.BlockSpec((HALO, width), lambda i: (jnp.minimum((i + 1) * blocks_per_group, last), col))
    return [cur, prev, nxt]


def _conv_mixer(u, dw, cb, cg):
    full = lambda shape: pl.BlockSpec(shape, lambda i: (0, 0))
    return pl.pallas_call(
        _conv_kernel,
        grid=(N_GROUPS,),
        in_specs=_halo_specs(2 * D_CONV, COL_CONV // (2 * D_CONV))
        + [full((CONV_WIDTH, D_CONV)), full((1, D_CONV)), full((1, D_CONV))],
        out_specs=pl.BlockSpec((GROUP, D_CONV), lambda i: (i, 0)),
        out_shape=jax.ShapeDtypeStruct((N_TOK, D_CONV), BF16),
        scratch_shapes=[pltpu.VMEM((GROUP + 2 * HALO, D_CONV), F32), pltpu.VMEM((GROUP, D_CONV), F32)],
        compiler_params=_params(("arbitrary",)),
        name="conv_mixer",
    )(u, u, u, dw, cb, cg)


def _pool_kernel(cur_ref, prev_ref, next_ref, pw_ref, ps_ref, o_ref, zp_ref):
    i = pl.program_id(0)
    has_prev, has_next = _halo_masks(i)
    zp_ref[0:HALO] = jnp.where(has_prev, prev_ref[...], 0.0)
    zp_ref[HALO:HALO + GROUP] = cur_ref[...]
    zp_ref[HALO + GROUP:] = jnp.where(has_next, next_ref[...], 0.0)

    is_lat = i >= CTX_GROUPS
    seq_len = jnp.where(is_lat, DEC_SEQ, SEQ)
    pos0 = jnp.where(is_lat, (i & (LAT_GROUPS_PER_SEQ - 1)) * GROUP, 0)
    pos = lax.broadcasted_iota(jnp.int32, (GROUP, 1), 0) + pos0
    for gi, w in enumerate(POOL_WINDOWS):
        cols = slice(gi * POOL_GROUP, (gi + 1) * POOL_GROUP)
        tot = zp_ref[pl.ds(HALO - w // 2, GROUP), cols]
        for d in range(1, w):
            tot = tot + zp_ref[pl.ds(HALO - w // 2 + d, GROUP), cols]
        lo = jnp.clip(pos - w // 2, 0, seq_len)
        hi = jnp.clip(pos - w // 2 + w, 0, seq_len)
        mean = tot / (hi - lo).astype(F32)
        d = (mean - cur_ref[:, cols]).astype(BF16)
        o_ref[:, cols] = (_dot(d, pw_ref[gi]) * ps_ref[:, cols]).astype(BF16)


def _pool_mixer(u, pw, ps):
    return pl.pallas_call(
        _pool_kernel,
        grid=(N_GROUPS,),
        in_specs=_halo_specs(D_POOL, COL_POOL // D_POOL)
        + [pl.BlockSpec((len(POOL_WINDOWS), POOL_GROUP, POOL_GROUP), lambda i: (0, 0, 0)),
           pl.BlockSpec((1, D_POOL), lambda i: (0, 0))],
        out_specs=pl.BlockSpec((GROUP, D_POOL), lambda i: (i, 0)),
        out_shape=jax.ShapeDtypeStruct((N_TOK, D_POOL), BF16),
        scratch_shapes=[pltpu.VMEM((GROUP + 2 * HALO, D_POOL), F32)],
        compiler_params=_params(("arbitrary",)),
        name="pool_mixer",
    )(u, u, u, pw, ps)


def _merge_kernel(x_ref, a_ref, b_ref, c_ref, ga_ref, gb_ref, gc_ref, gt_ref,
                  wa_ref, wb_ref, wc_ref, wo_ref, o_ref):
    merged = jax.nn.sigmoid(ga_ref[...]) * _dot(a_ref[...], wa_ref[...])
    merged += jax.nn.sigmoid(gb_ref[...]) * _dot(b_ref[...], wb_ref[...])
    merged += jax.nn.sigmoid(gc_ref[...]) * _dot(c_ref[...], wc_ref[...])
    o_ref[...] = x_ref[...] + gt_ref[...] * _dot(merged.astype(BF16), wo_ref[...])


def _merge(x, attn, conv, pool, u, gt, wa, wb, wc, wo, l):
    tok = lambda width, col=0: pl.BlockSpec((MERGE_TILE, width), lambda i: (i, col))
    gate0 = COL_GATE // D_MODEL
    weight = lambda rows: pl.BlockSpec((None, rows, D_MODEL), lambda i: (l, 0, 0),
                                       pipeline_mode=pl.Buffered(1))
    return pl.pallas_call(
        _merge_kernel,
        grid=(N_TOK // MERGE_TILE,),
        in_specs=[
            tok(D_MODEL), tok(D_ATTN), tok(D_CONV), tok(D_POOL),
            tok(D_MODEL, gate0), tok(D_MODEL, gate0 + 1), tok(D_MODEL, gate0 + 2),
            pl.BlockSpec((None, 1, D_MODEL), lambda i: (i * (MERGE_TILE // GROUP), 0, 0)),
            weight(D_ATTN), weight(D_CONV), weight(D_POOL), weight(D_MODEL),
        ],
        out_specs=tok(D_MODEL),
        out_shape=jax.ShapeDtypeStruct((N_TOK, D_MODEL), F32),
        compiler_params=_params(("arbitrary",)),
        name="merge",
    )(x, attn, conv, pool, u, u, u, gt, wa, wb, wc, wo)


def _ffn_tiles(w, pad_axis):
    pad = [(0, 0)] * w.ndim
    pad[pad_axis] = (0, D_FF_PAD - D_FF)
    w = jnp.pad(w.astype(BF16), pad)
    if pad_axis == 3:
        return w.reshape(DEPTH, 2, D_MODEL, N_FF_TILES, FF_TILE).transpose(0, 1, 3, 2, 4)
    return w.reshape(DEPTH, 2, N_FF_TILES, FF_TILE, D_MODEL)


def kernel(x_prompt, x_sample, cache_k, cache_v, c, c_ctx, w_mod, b_mod, norm_g, ffn_w1, ffn_w3, ffn_w2,
           w_in, qk_g, rpb, w_br_a, conv_dw, conv_b, conv_g, w_br_b, pool_w, pool_scale, w_br_c, w_out):
    x = jnp.concatenate([x_prompt.reshape(N_CTX, D_MODEL), x_sample.reshape(N_LAT, D_MODEL)], axis=0)

    cpad = jnp.concatenate([c_ctx[None, :], c, jnp.zeros((8 - 1 - DEC_BATCH, D_MODEL), F32)], axis=0)
    mod = _modulation(cpad, w_mod, b_mod)
    group_row = jnp.concatenate([jnp.zeros((CTX_GROUPS,), jnp.int32),
                                 1 + jnp.arange(N_LAT // GROUP, dtype=jnp.int32) // LAT_GROUPS_PER_SEQ])
    mod = mod[:, group_row].reshape(DEPTH, N_GROUPS, N_MOD, 1, D_MODEL).transpose(0, 2, 1, 3, 4)

    w1t = _ffn_tiles(ffn_w1, 3)
    w3t = _ffn_tiles(ffn_w3, 3)
    w2t = _ffn_tiles(ffn_w2, 2)
    split = 3 * D_ATTN + 2 * D_CONV
    w_in_b = jnp.concatenate([w_in[:, :, :split], w_in[:, :, split + D_POOL:], w_in[:, :, split:split + D_POOL]],
                             axis=2).astype(BF16)
    wa_b, wb_b, wc_b, wo_b = (w.astype(BF16) for w in (w_br_a, w_br_b, w_br_c, w_out))
    pool_w_b = pool_w.astype(BF16)
    qk_pair = jnp.tile(qk_g, (1, 1, 2)).reshape(DEPTH, 2, 1, LANES)
    ck = cache_k.reshape(DEC_BATCH, DEPTH, PAST_LEN, D_ATTN)
    cv = cache_v.reshape(DEC_BATCH, DEPTH, PAST_LEN, D_ATTN)

    new_k = jnp.zeros((BATCH, DEPTH, SEQ, D_ATTN), F32)
    new_v = jnp.zeros((BATCH, DEPTH, SEQ, D_ATTN), F32)

    for l in range(DEPTH):
        m = mod[l]
        ng = norm_g[l][:, None, :]
        x = _ffn(x, ng[0], m[0], m[1], m[2], w1t, w3t, w2t, l, 0)
        u = _inproj(x, ng[1], m[3], m[4], w_in_b, l)
        attn, new_k, new_v = _ctx_attention(u, qk_pair[l, 0], qk_pair[l, 1], new_k, new_v, l)
        attn = _lat_attention(u, ck, cv, _bias_table(rpb[l]), qk_pair[l, 0], qk_pair[l, 1], attn, l)
        conv = _conv_mixer(u, conv_dw[l], conv_b[l][None, :], conv_g[l][None, :])
        pool = _pool_mixer(u, pool_w_b[l], pool_scale[l][None, :])
        x = _merge(x, attn, conv, pool, u, m[5], wa_b, wb_b, wc_b, wo_b, l)
        x = _ffn(x, ng[2], m[6], m[7], m[8], w1t, w3t, w2t, l, 1)

    y_prompt = x[:N_CTX].reshape(BATCH, SEQ, D_MODEL)
    y_sample = x[N_CTX:].reshape(DEC_BATCH, DEC_SEQ, D_MODEL)
    shape = (BATCH, DEPTH, SEQ, N_HEADS, HEAD_DIM)
    return y_prompt, y_sample, new_k.reshape(shape), new_v.reshape(shape)
```

```python
import functools

import jax
import jax.numpy as jnp
from jax import lax
from jax.experimental import pallas as pl
from jax.experimental.pallas import tpu as pltpu

D_MODEL = 2048
BATCH = 32
SEQ = 256
DEPTH = 4
DEC_BATCH = 4
DEC_SEQ = 1024
PAST_LEN = 512
GRID_W = 64
GRID_ROWS = DEC_SEQ // GRID_W
D_ATTN = D_MODEL // 2
N_HEADS = 16
HEAD_DIM = D_ATTN // N_HEADS
WIN_H = 8
WIN_W = 16
D_CONV = D_MODEL // 4
CONV_WIDTH = 31
D_POOL = D_MODEL // 4
POOL_WINDOWS = (2, 4, 8, 16)
POOL_GROUP = D_POOL // len(POOL_WINDOWS)
D_FF = ((8 * D_MODEL // 3 + 127) // 128) * 128
N_MOD = 9
EPS = 1e-6
IN_WIDTH = 3 * D_ATTN + 2 * D_CONV + D_POOL + 3 * D_MODEL

N_CTX = BATCH * SEQ
N_LAT = DEC_BATCH * DEC_SEQ
N_TOK = N_CTX + N_LAT
GROUP = 256
N_GROUPS = N_TOK // GROUP
CTX_GROUPS = N_CTX // GROUP
LAT_GROUPS_PER_SEQ = DEC_SEQ // GROUP

LANES = 128
SUBLANES = 8
HEAD_PAIRS = N_HEADS // 2
HALO = 16

COL_Q = 0
COL_K = D_ATTN
COL_V = 2 * D_ATTN
COL_CONV = 3 * D_ATTN
COL_POOL = COL_CONV + 2 * D_CONV
COL_GATE = COL_POOL + D_POOL

FF_TILE = 512
N_FF_TILES = D_FF // FF_TILE
FF_TAIL = D_FF - N_FF_TILES * FF_TILE
TOK_TILE = 512
IN_TOK_TILE = 1024
IN_TILE = 768
MERGE_TILE = 256
GATE_BLK = 1536
MOD_TILE = 1024

Q_ROWS = 4
Q_BLK = Q_ROWS * GRID_W
N_QBLK = GRID_ROWS // Q_ROWS
KEY_ROWS = 12
KEY_BLK = KEY_ROWS * GRID_W
NEG = -1e30

VMEM_LIMIT = 48 * 1024 * 1024

F32 = jnp.float32
BF16 = jnp.bfloat16


def _params(sem, vmem=VMEM_LIMIT):
    return pltpu.CompilerParams(dimension_semantics=sem, vmem_limit_bytes=vmem)


def _dot(a, b):
    return jnp.dot(a, b, preferred_element_type=F32)


def _dot_nt(a, b):
    return lax.dot_general(a, b, (((1,), (1,)), ((), ())), preferred_element_type=F32)


def _adaln(x, g, sc, sh):
    ms = jnp.mean(x * x, axis=-1, keepdims=True)
    y = (x * lax.rsqrt(ms + EPS)) * g
    return y * (1.0 + sc) + sh


def _mod_kernel(c_ref, w_ref, b_ref, o_ref):
    c = c_ref[...]
    s = (c * jax.nn.sigmoid(c)).astype(BF16)
    o_ref[...] = _dot(s, w_ref[...].astype(BF16)) + b_ref[...]


def _modulation(cpad, w_mod, b_mod):
    width = N_MOD * D_MODEL
    return pl.pallas_call(
        _mod_kernel,
        grid=(DEPTH, width // MOD_TILE),
        in_specs=[
            pl.BlockSpec((8, D_MODEL), lambda l, n: (0, 0)),
            pl.BlockSpec((None, D_MODEL, MOD_TILE), lambda l, n: (l, 0, n)),
            pl.BlockSpec((None, 1, MOD_TILE), lambda l, n: (l, 0, n)),
        ],
        out_specs=pl.BlockSpec((None, 8, MOD_TILE), lambda l, n: (l, 0, n)),
        out_shape=jax.ShapeDtypeStruct((DEPTH, 8, width), F32),
        compiler_params=_params(("arbitrary", "arbitrary")),
        name="modulation",
    )(cpad, w_mod, b_mod.reshape(DEPTH, 1, width))


def _ffn_kernel(x_ref, ng_ref, sh_ref, sc_ref, gt_ref, w1_ref, w3_ref, w2_ref,
                w1t_ref, w3t_ref, w2t_ref, o_ref, h_ref):
    @pl.when(pl.program_id(1) == 0)
    def _():
        x = x_ref[...]
        h_ref[...] = _adaln(x, ng_ref[...], sc_ref[...], sh_ref[...]).astype(BF16)
        o_ref[...] = x

    def hidden_slice(w1, w3, w2):
        h = h_ref[...]
        a = _dot(h, w1)
        b = _dot(h, w3)
        g = (a * jax.nn.sigmoid(a) * b).astype(BF16)
        o_ref[...] += (0.5 * gt_ref[...]) * _dot(g, w2)

    hidden_slice(w1_ref[...], w3_ref[...], w2_ref[...])

    @pl.when(pl.program_id(1) == N_FF_TILES - 1)
    def _():
        hidden_slice(w1t_ref[...], w3t_ref[...], w2t_ref[...])


def _ffn(x, ng, sh, sc, gt, w1, w3, w2, w1t, w3t, w2t, l, s):
    step = TOK_TILE // GROUP
    mod_spec = pl.BlockSpec((None, 1, D_MODEL), lambda i, j: (i * step, 0, 0))
    up_tail = pl.BlockSpec((None, None, D_MODEL, FF_TAIL), lambda i, j: (l, s, 0, 0),
                           pipeline_mode=pl.Buffered(1))
    return pl.pallas_call(
        _ffn_kernel,
        grid=(N_TOK // TOK_TILE, N_FF_TILES),
        in_specs=[
            pl.BlockSpec((TOK_TILE, D_MODEL), lambda i, j: (i, 0)),
            pl.BlockSpec((1, D_MODEL), lambda i, j: (0, 0)),
            mod_spec, mod_spec, mod_spec,
            pl.BlockSpec((None, None, D_MODEL, FF_TILE), lambda i, j: (l, s, 0, j)),
            pl.BlockSpec((None, None, D_MODEL, FF_TILE), lambda i, j: (l, s, 0, j)),
            pl.BlockSpec((None, None, FF_TILE, D_MODEL), lambda i, j: (l, s, j, 0)),
            up_tail, up_tail,
            pl.BlockSpec((None, None, FF_TAIL, D_MODEL), lambda i, j: (l, s, 0, 0),
                         pipeline_mode=pl.Buffered(1)),
        ],
        out_specs=pl.BlockSpec((TOK_TILE, D_MODEL), lambda i, j: (i, 0)),
        out_shape=jax.ShapeDtypeStruct((N_TOK, D_MODEL), F32),
        scratch_shapes=[pltpu.VMEM((TOK_TILE, D_MODEL), BF16)],
        compiler_params=_params(("arbitrary", "arbitrary")),
        name="ffn",
    )(x, ng, sh, sc, gt, w1, w3, w2, w1t, w3t, w2t)


def _inproj_kernel(x_ref, ng_ref, sh_ref, sc_ref, w_ref, o_ref, h_ref):
    @pl.when(pl.program_id(1) == 0)
    def _():
        h_ref[...] = _adaln(x_ref[...], ng_ref[...], sc_ref[...], sh_ref[...]).astype(BF16)

    o_ref[...] = _dot(h_ref[...], w_ref[...])


def _inproj(x, ng, sh, sc, w_in, l):
    step = IN_TOK_TILE // GROUP
    mod_spec = pl.BlockSpec((None, 1, D_MODEL), lambda i, j: (i * step, 0, 0))
    return pl.pallas_call(
        _inproj_kernel,
        grid=(N_TOK // IN_TOK_TILE, IN_WIDTH // IN_TILE),
        in_specs=[
            pl.BlockSpec((IN_TOK_TILE, D_MODEL), lambda i, j: (i, 0)),
            pl.BlockSpec((1, D_MODEL), lambda i, j: (0, 0)),
            mod_spec, mod_spec,
            pl.BlockSpec((None, D_MODEL, IN_TILE), lambda i, j: (l, 0, j)),
        ],
        out_specs=pl.BlockSpec((IN_TOK_TILE, IN_TILE), lambda i, j: (i, j)),
        out_shape=jax.ShapeDtypeStruct((N_TOK, IN_WIDTH), F32),
        scratch_shapes=[pltpu.VMEM((IN_TOK_TILE, D_MODEL), BF16)],
        compiler_params=_params(("arbitrary", "arbitrary")),
        name="inproj",
    )(x, ng, sh, sc, w_in)


def _pair_rmsnorm(x, g, first):
    x2 = x * x
    s0 = jnp.sum(jnp.where(first, x2, 0.0), axis=-1, keepdims=True)
    s1 = jnp.sum(jnp.where(first, 0.0, x2), axis=-1, keepdims=True)
    inv = 1.0 / HEAD_DIM
    r = jnp.where(first, lax.rsqrt(s0 * inv + EPS), lax.rsqrt(s1 * inv + EPS))
    return (x * r) * g


def _first_head_mask():
    return lax.broadcasted_iota(jnp.int32, (1, LANES), 1) < HEAD_DIM


def _ctx_attn_kernel(q_ref, k_ref, v_ref, qg_ref, kg_ref, nk_in, nv_in, o_ref, nk_ref, nv_ref):
    del nk_in, nv_in
    first = _first_head_mask()
    for p in range(HEAD_PAIRS):
        cols = slice(p * LANES, (p + 1) * LANES)
        qn = _pair_rmsnorm(q_ref[:, cols], qg_ref[...], first) * (HEAD_DIM ** -0.5)
        kn = _pair_rmsnorm(k_ref[:, cols], kg_ref[...], first)
        v = v_ref[:, cols]
        nk_ref[:, cols] = kn
        nv_ref[:, cols] = v
        kb = kn.astype(BF16)
        vb = v.astype(BF16)
        outs = []
        for hh in range(2):
            sel = first if hh == 0 else jnp.logical_not(first)
            qm = jnp.where(sel, qn, 0.0).astype(BF16)
            s = _dot_nt(qm, kb)
            m = jnp.max(s, axis=-1, keepdims=True)
            e = jnp.exp(s - m)
            den = jnp.sum(e, axis=-1, keepdims=True)
            outs.append(_dot(e.astype(BF16), vb) / den)
        o_ref[:, cols] = jnp.where(first, outs[0], outs[1]).astype(BF16)


def _ctx_attention(u, qg, kg, new_k, new_v, l):
    qkv = lambda c: pl.BlockSpec((SEQ, D_ATTN), lambda b: (b, c))
    cache_spec = pl.BlockSpec((None, None, SEQ, D_ATTN), lambda b: (b, l, 0, 0))
    gain_spec = pl.BlockSpec((1, LANES), lambda b: (0, 0))
    any_spec = pl.BlockSpec(memory_space=pl.ANY)
    return pl.pallas_call(
        _ctx_attn_kernel,
        grid=(BATCH,),
        in_specs=[qkv(COL_Q // D_ATTN), qkv(COL_K // D_ATTN), qkv(COL_V // D_ATTN),
                  gain_spec, gain_spec, any_spec, any_spec],
        out_specs=[pl.BlockSpec((SEQ, D_ATTN), lambda b: (b, 0)), cache_spec, cache_spec],
        out_shape=[jax.ShapeDtypeStruct((N_TOK, D_ATTN), BF16),
                   jax.ShapeDtypeStruct(new_k.shape, F32),
                   jax.ShapeDtypeStruct(new_v.shape, F32)],
        input_output_aliases={5: 1, 6: 2},
        compiler_params=_params(("arbitrary",)),
        name="ctx_attention",
    )(u, u, u, qg, kg, new_k, new_v)


def _window_start(r):
    return min(max(r - WIN_H // 2, 0), GRID_ROWS - WIN_H)


def _key_window_row(blk):
    return min(max(_window_start(blk * Q_ROWS), 0), GRID_ROWS - KEY_ROWS)


def _bias_kernel(rpb_ref, o_ref, t_ref):
    h = pl.program_id(0)
    n_dc = 2 * WIN_W - 1
    n_dr = 2 * WIN_H - 1
    lane = lax.broadcasted_iota(jnp.int32, (GRID_W, LANES), 1)
    qc = lax.broadcasted_iota(jnp.int32, (GRID_W, LANES), 0)
    kc = lane & (GRID_W - 1)
    dc = kc - qc
    col_start = jnp.clip(qc - WIN_W // 2, 0, GRID_W - WIN_W)
    col_in = (kc >= col_start) & (kc < col_start + WIN_W)
    base = h * (n_dr * n_dc)
    for dr in range(n_dr):
        row = base + dr * n_dc
        t = jnp.full((GRID_W, LANES), rpb_ref[row], F32)
        for j in range(1, n_dc - 1):
            t = jnp.where(dc == j - (WIN_W - 1), rpb_ref[row + j], t)
        t = jnp.where(dc >= WIN_W - 1, rpb_ref[row + n_dc - 1], t)
        t_ref[dr] = jnp.where(col_in, t, NEG)

    left = lane < GRID_W
    neg = jnp.full((GRID_W, LANES), NEG, F32)
    for blk in range(N_QBLK):
        ws = _key_window_row(blk)
        for qi in range(Q_ROWS):
            qr = blk * Q_ROWS + qi
            st = _window_start(qr)

            def tile(kj):
                kr = ws + kj
                if st <= kr < st + WIN_H:
                    return t_ref[kr - qr + WIN_H - 1]
                return None

            for g in range(KEY_ROWS // 2):
                a, b = tile(2 * g), tile(2 * g + 1)
                if a is None and b is None:
                    val = neg
                else:
                    val = jnp.where(left, neg if a is None else a, neg if b is None else b)
                o_ref[blk, qi * GRID_W:(qi + 1) * GRID_W, g * LANES:(g + 1) * LANES] = val


def _bias_table(rpb_l):
    n = (2 * WIN_H - 1) * (2 * WIN_W - 1)
    return pl.pallas_call(
        _bias_kernel,
        grid=(N_HEADS,),
        in_specs=[pl.BlockSpec(memory_space=pltpu.SMEM)],
        out_specs=pl.BlockSpec((None, N_QBLK, Q_BLK, KEY_BLK), lambda h: (h, 0, 0, 0)),
        out_shape=jax.ShapeDtypeStruct((N_HEADS, N_QBLK, Q_BLK, KEY_BLK), F32),
        scratch_shapes=[pltpu.VMEM((2 * WIN_H - 1, GRID_W, LANES), F32)],
        compiler_params=_params(("arbitrary",)),
        name="bias_table",
    )(rpb_l.reshape(N_HEADS * n))


def _lat_attn_kernel(q_ref, k_ref, v_ref, ck_ref, cv_ref, bias_ref, qg_ref, kg_ref, o_in, o_ref):
    del o_in
    blk = pl.program_id(2)
    first = _first_head_mask()
    lo_row = _key_window_row(0) * GRID_W
    hi_row = _key_window_row(N_QBLK - 1) * GRID_W
    ws = pl.multiple_of(jnp.where(blk < N_QBLK // 2, lo_row, hi_row), GRID_W)
    qn = _pair_rmsnorm(q_ref[...], qg_ref[...], first) * (HEAD_DIM ** -0.5)
    kb = _pair_rmsnorm(k_ref[pl.ds(ws, KEY_BLK), :], kg_ref[...], first).astype(BF16)
    vb = v_ref[pl.ds(ws, KEY_BLK), :].astype(BF16)
    ckb = ck_ref[...].astype(BF16)
    cvb = cv_ref[...].astype(BF16)
    outs = []
    for hh in range(2):
        sel = first if hh == 0 else jnp.logical_not(first)
        qm = jnp.where(sel, qn, 0.0).astype(BF16)
        s_loc = _dot_nt(qm, kb) + bias_ref[hh]
        s_ctx = _dot_nt(qm, ckb)
        m = jnp.maximum(jnp.max(s_loc, axis=-1, keepdims=True), jnp.max(s_ctx, axis=-1, keepdims=True))
        e_loc = jnp.exp(s_loc - m)
        e_ctx = jnp.exp(s_ctx - m)
        den = jnp.sum(e_loc, axis=-1, keepdims=True) + jnp.sum(e_ctx, axis=-1, keepdims=True)
        acc = _dot(e_loc.astype(BF16), vb) + _dot(e_ctx.astype(BF16), cvb)
        outs.append(acc / den)
    o_ref[...] = jnp.where(first, outs[0], outs[1]).astype(BF16)


def _lat_attention(u, cache_k, cache_v, bias, qg, kg, attn, l):
    q_row0 = N_CTX // Q_BLK
    seq0 = N_CTX // DEC_SEQ
    gain_spec = pl.BlockSpec((1, LANES), lambda b, p, r: (0, 0))
    cache_spec = pl.BlockSpec((None, None, PAST_LEN, LANES), lambda b, p, r: (b, l, 0, p))
    return pl.pallas_call(
        _lat_attn_kernel,
        grid=(DEC_BATCH, HEAD_PAIRS, N_QBLK),
        in_specs=[
            pl.BlockSpec((Q_BLK, LANES), lambda b, p, r: (q_row0 + b * N_QBLK + r, COL_Q // LANES + p)),
            pl.BlockSpec((DEC_SEQ, LANES), lambda b, p, r: (seq0 + b, COL_K // LANES + p)),
            pl.BlockSpec((DEC_SEQ, LANES), lambda b, p, r: (seq0 + b, COL_V // LANES + p)),
            cache_spec, cache_spec,
            pl.BlockSpec((2, None, Q_BLK, KEY_BLK), lambda b, p, r: (p, r, 0, 0)),
            gain_spec, gain_spec,
            pl.BlockSpec(memory_space=pl.ANY),
        ],
        out_specs=pl.BlockSpec((Q_BLK, LANES), lambda b, p, r: (q_row0 + b * N_QBLK + r, p)),
        out_shape=jax.ShapeDtypeStruct((N_TOK, D_ATTN), BF16),
        input_output_aliases={8: 0},
        compiler_params=_params(("arbitrary", "arbitrary", "arbitrary")),
        name="lat_attention",
    )(u, u, u, cache_k, cache_v, bias, qg, kg, attn)


def _halo_masks(i):
    is_lat = i >= CTX_GROUPS
    j = i & (LAT_GROUPS_PER_SEQ - 1)
    return is_lat & (j != 0), is_lat & (j != LAT_GROUPS_PER_SEQ - 1)


def _conv_kernel(cur_ref, prev_ref, next_ref, dw_ref, cb_ref, cg_ref, o_ref, hp_ref, acc_ref, sh_ref):
    has_prev, has_next = _halo_masks(pl.program_id(0))

    def glu(z):
        return z[:, :D_CONV] * jax.nn.sigmoid(z[:, D_CONV:])

    hp_ref[0:HALO] = jnp.where(has_prev, glu(prev_ref[...]), 0.0)
    hp_ref[HALO:HALO + GROUP] = glu(cur_ref[...])
    hp_ref[HALO + GROUP:] = jnp.where(has_next, glu(next_ref[...]), 0.0)

    rows = 64
    off = HALO - CONV_WIDTH // 2
    span = GROUP + 2 * HALO - SUBLANES
    for lc in range(D_CONV // LANES):
        cols = slice(lc * LANES, (lc + 1) * LANES)
        for r in range(SUBLANES):
            sh_ref[r, 0:span] = hp_ref[pl.ds(r, span), cols]
        accs = [jnp.zeros((rows, LANES), F32) for _ in range(GROUP // rows)]
        for j in range(CONV_WIDTH):
            r = (off + j) % SUBLANES
            w = dw_ref[j:j + 1, cols]
            for rc in range(GROUP // rows):
                base = rc * rows + (off + j) - r
                accs[rc] = accs[rc] + sh_ref[r, base:base + rows] * w
        for rc in range(GROUP // rows):
            acc_ref[rc * rows:(rc + 1) * rows, cols] = accs[rc] + cb_ref[:, cols]

    y = acc_ref[...]
    ms = jnp.mean(y * y, axis=-1, keepdims=True)
    y = (y * lax.rsqrt(ms + EPS)) * cg_ref[...]
    o_ref[...] = (y * jax.nn.sigmoid(y)).astype(BF16)


def _halo_specs(width, col):
    blocks_per_group = GROUP // HALO
    last = N_TOK // HALO - 1
    cur = pl.BlockSpec((GROUP, width), lambda i: (i, col))
    prev = pl.BlockSpec((HALO, width), lambda i: (jnp.maximum(i * blocks_per_group - 1, 0), col))
    nxt = pl.BlockSpec((HALO, width), lambda i: (jnp.minimum((i + 1) * blocks_per_group, last), col))
    return [cur, prev, nxt]


def _conv_mixer(u, dw, cb, cg):
    full = lambda shape: pl.BlockSpec(shape, lambda i: (0, 0))
    return pl.pallas_call(
        _conv_kernel,
        grid=(N_GROUPS,),
        in_specs=_halo_specs(2 * D_CONV, COL_CONV // (2 * D_CONV))
        + [full((CONV_WIDTH, D_CONV)), full((1, D_CONV)), full((1, D_CONV))],
        out_specs=pl.BlockSpec((GROUP, D_CONV), lambda i: (i, 0)),
        out_shape=jax.ShapeDtypeStruct((N_TOK, D_CONV), BF16),
        scratch_shapes=[pltpu.VMEM((GROUP + 2 * HALO, D_CONV), F32), pltpu.VMEM((GROUP, D_CONV), F32),
                        pltpu.VMEM((SUBLANES, GROUP + 2 * HALO, LANES), F32)],
        compiler_params=_params(("arbitrary",)),
        name="conv_mixer",
    )(u, u, u, dw, cb, cg)


def _pool_kernel(cur_ref, prev_ref, next_ref, pw_ref, ps_ref, o_ref, zp_ref):
    i = pl.program_id(0)
    has_prev, has_next = _halo_masks(i)
    zp_ref[0:HALO] = jnp.where(has_prev, prev_ref[...], 0.0)
    zp_ref[HALO:HALO + GROUP] = cur_ref[...]
    zp_ref[HALO + GROUP:] = jnp.where(has_next, next_ref[...], 0.0)

    is_lat = i >= CTX_GROUPS
    seq_len = jnp.where(is_lat, DEC_SEQ, SEQ)
    pos0 = jnp.where(is_lat, (i & (LAT_GROUPS_PER_SEQ - 1)) * GROUP, 0)
    pos = lax.broadcasted_iota(jnp.int32, (GROUP, 1), 0) + pos0
    for gi, w in enumerate(POOL_WINDOWS):
        cols = slice(gi * POOL_GROUP, (gi + 1) * POOL_GROUP)
        tot = zp_ref[pl.ds(HALO - w // 2, GROUP), cols]
        for d in range(1, w):
            tot = tot + zp_ref[pl.ds(HALO - w // 2 + d, GROUP), cols]
        lo = jnp.clip(pos - w // 2, 0, seq_len)
        hi = jnp.clip(pos - w // 2 + w, 0, seq_len)
        mean = tot / (hi - lo).astype(F32)
        d = (mean - cur_ref[:, cols]).astype(BF16)
        o_ref[:, cols] = (_dot(d, pw_ref[gi]) * ps_ref[:, cols]).astype(BF16)


def _pool_mixer(u, pw, ps):
    return pl.pallas_call(
        _pool_kernel,
        grid=(N_GROUPS,),
        in_specs=_halo_specs(D_POOL, COL_POOL // D_POOL)
        + [pl.BlockSpec((len(POOL_WINDOWS), POOL_GROUP, POOL_GROUP), lambda i: (0, 0, 0)),
           pl.BlockSpec((1, D_POOL), lambda i: (0, 0))],
        out_specs=pl.BlockSpec((GROUP, D_POOL), lambda i: (i, 0)),
        out_shape=jax.ShapeDtypeStruct((N_TOK, D_POOL), BF16),
        scratch_shapes=[pltpu.VMEM((GROUP + 2 * HALO, D_POOL), F32)],
        compiler_params=_params(("arbitrary",)),
        name="pool_mixer",
    )(u, u, u, pw, ps)


def _merge_kernel(x_ref, a_ref, b_ref, c_ref, g0_ref, g1_ref, g2_ref, g3_ref, gt_ref,
                  wa_ref, wb_ref, wc_ref, wo_ref, o_ref):
    gate_refs = (g0_ref, g1_ref, g2_ref, g3_ref)

    def gate(k):
        parts = []
        lo, hi = k * D_MODEL, (k + 1) * D_MODEL
        while lo < hi:
            blk, start = divmod(lo, GATE_BLK)
            stop = min(GATE_BLK, start + hi - lo)
            parts.append(gate_refs[blk][:, start:stop])
            lo += stop - start
        return jax.nn.sigmoid(jnp.concatenate(parts, axis=1))

    merged = gate(0) * _dot(a_ref[...], wa_ref[...])
    merged += gate(1) * _dot(b_ref[...], wb_ref[...])
    merged += gate(2) * _dot(c_ref[...], wc_ref[...])
    o_ref[...] = x_ref[...] + gt_ref[...] * _dot(merged.astype(BF16), wo_ref[...])


def _merge(x, attn, conv, pool, u, gt, wa, wb, wc, wo, l):
    tok = lambda width, col=0: pl.BlockSpec((MERGE_TILE, width), lambda i: (i, col))
    gate0 = COL_GATE // GATE_BLK
    weight = lambda rows: pl.BlockSpec((None, rows, D_MODEL), lambda i: (l, 0, 0),
                                       pipeline_mode=pl.Buffered(1))
    return pl.pallas_call(
        _merge_kernel,
        grid=(N_TOK // MERGE_TILE,),
        in_specs=[
            tok(D_MODEL), tok(D_ATTN), tok(D_CONV), tok(D_POOL),
            tok(GATE_BLK, gate0), tok(GATE_BLK, gate0 + 1), tok(GATE_BLK, gate0 + 2), tok(GATE_BLK, gate0 + 3),
            pl.BlockSpec((None, 1, D_MODEL), lambda i: (i * (MERGE_TILE // GROUP), 0, 0)),
            weight(D_ATTN), weight(D_CONV), weight(D_POOL), weight(D_MODEL),
        ],
        out_specs=tok(D_MODEL),
        out_shape=jax.ShapeDtypeStruct((N_TOK, D_MODEL), F32),
        compiler_params=_params(("arbitrary",)),
        name="merge",
    )(x, attn, conv, pool, u, u, u, u, gt, wa, wb, wc, wo)


def kernel(x_prompt, x_sample, cache_k, cache_v, c, c_ctx, w_mod, b_mod, norm_g, ffn_w1, ffn_w3, ffn_w2,
           w_in, qk_g, rpb, w_br_a, conv_dw, conv_b, conv_g, w_br_b, pool_w, pool_scale, w_br_c, w_out):
    x = jnp.concatenate([x_prompt.reshape(N_CTX, D_MODEL), x_sample.reshape(N_LAT, D_MODEL)], axis=0)

    cpad = jnp.concatenate([c_ctx[None, :], c, jnp.zeros((8 - 1 - DEC_BATCH, D_MODEL), F32)], axis=0)
    mod = _modulation(cpad, w_mod, b_mod)
    group_row = jnp.concatenate([jnp.zeros((CTX_GROUPS,), jnp.int32),
                                 1 + jnp.arange(N_LAT // GROUP, dtype=jnp.int32) // LAT_GROUPS_PER_SEQ])
    mod = mod[:, group_row].reshape(DEPTH, N_GROUPS, N_MOD, 1, D_MODEL).transpose(0, 2, 1, 3, 4)

    w1_b, w3_b, w2_b, w_in_b = (w.astype(BF16) for w in (ffn_w1, ffn_w3, ffn_w2, w_in))
    tail0 = N_FF_TILES * FF_TILE
    w1t, w3t, w2t = w1_b[..., tail0:], w3_b[..., tail0:], w2_b[:, :, tail0:, :]
    wa_b, wb_b, wc_b, wo_b = (w.astype(BF16) for w in (w_br_a, w_br_b, w_br_c, w_out))
    pool_w_b = pool_w.astype(BF16)
    qk_pair = jnp.tile(qk_g, (1, 1, 2)).reshape(DEPTH, 2, 1, LANES)
    ck = cache_k.reshape(DEC_BATCH, DEPTH, PAST_LEN, D_ATTN)
    cv = cache_v.reshape(DEC_BATCH, DEPTH, PAST_LEN, D_ATTN)

    new_k = jnp.zeros((BATCH, DEPTH, SEQ, D_ATTN), F32)
    new_v = jnp.zeros((BATCH, DEPTH, SEQ, D_ATTN), F32)

    for l in range(DEPTH):
        m = mod[l]
        ng = norm_g[l][:, None, :]
        ffn_w = (w1_b, w3_b, w2_b, w1t, w3t, w2t)
        x = _ffn(x, ng[0], m[0], m[1], m[2], *ffn_w, l, 0)
        u = _inproj(x, ng[1], m[3], m[4], w_in_b, l)
        attn, new_k, new_v = _ctx_attention(u, qk_pair[l, 0], qk_pair[l, 1], new_k, new_v, l)
        attn = _lat_attention(u, ck, cv, _bias_table(rpb[l]), qk_pair[l, 0], qk_pair[l, 1], attn, l)
        conv = _conv_mixer(u, conv_dw[l], conv_b[l][None, :], conv_g[l][None, :])
        pool = _pool_mixer(u, pool_w_b[l], pool_scale[l][None, :])
        x = _merge(x, attn, conv, pool, u, m[5], wa_b, wb_b, wc_b, wo_b, l)
        x = _ffn(x, ng[2], m[6], m[7], m[8], *ffn_w, l, 1)

    y_prompt = x[:N_CTX].reshape(BATCH, SEQ, D_MODEL)
    y_sample = x[N_CTX:].reshape(DEC_BATCH, DEC_SEQ, D_MODEL)
    shape = (BATCH, DEPTH, SEQ, N_HEADS, HEAD_DIM)
    return y_prompt, y_sample, new_k.reshape(shape), new_v.reshape(shape)
```

```python
import functools

import jax
import jax.numpy as jnp
from jax import lax
from jax.experimental import pallas as pl
from jax.experimental.pallas import tpu as pltpu

D_MODEL = 2048
BATCH = 32
SEQ = 256
DEPTH = 4
DEC_BATCH = 4
DEC_SEQ = 1024
PAST_LEN = 512
GRID_W = 64
GRID_ROWS = DEC_SEQ // GRID_W
D_ATTN = D_MODEL // 2
N_HEADS = 16
HEAD_DIM = D_ATTN // N_HEADS
WIN_H = 8
WIN_W = 16
D_CONV = D_MODEL // 4
CONV_WIDTH = 31
D_POOL = D_MODEL // 4
POOL_WINDOWS = (2, 4, 8, 16)
POOL_GROUP = D_POOL // len(POOL_WINDOWS)
D_FF = ((8 * D_MODEL // 3 + 127) // 128) * 128
N_MOD = 9
EPS = 1e-6
IN_WIDTH = 3 * D_ATTN + 2 * D_CONV + D_POOL + 3 * D_MODEL

N_CTX = BATCH * SEQ
N_LAT = DEC_BATCH * DEC_SEQ
N_TOK = N_CTX + N_LAT
GROUP = 256
N_GROUPS = N_TOK // GROUP
CTX_GROUPS = N_CTX // GROUP
LAT_GROUPS_PER_SEQ = DEC_SEQ // GROUP

LANES = 128
SUBLANES = 8
HEAD_PAIRS = N_HEADS // 2
HALO = 16

COL_Q = 0
COL_K = D_ATTN
COL_V = 2 * D_ATTN
COL_CONV = 3 * D_ATTN
COL_POOL = COL_CONV + 2 * D_CONV
COL_GATE = COL_POOL + D_POOL

FF_TILE = 768
N_FF_TILES = D_FF // FF_TILE
FF_TAIL = D_FF - N_FF_TILES * FF_TILE
TOK_TILE = 512
IN_TOK_TILE = 1024
IN_TILE = 1536
MERGE_TILE = 256
GATE_BLK = 1536
MOD_TILE = 1024

Q_ROWS = 4
Q_BLK = Q_ROWS * GRID_W
N_QBLK = GRID_ROWS // Q_ROWS
KEY_ROWS = 12
KEY_BLK = KEY_ROWS * GRID_W
NEG = -1e30

VMEM_LIMIT = 48 * 1024 * 1024
BIG_VMEM_LIMIT = 56 * 1024 * 1024

F32 = jnp.float32
BF16 = jnp.bfloat16


def _params(sem, vmem=VMEM_LIMIT):
    return pltpu.CompilerParams(dimension_semantics=sem, vmem_limit_bytes=vmem)


def _dot(a, b):
    return jnp.dot(a, b, preferred_element_type=F32)


def _dot_nt(a, b):
    return lax.dot_general(a, b, (((1,), (1,)), ((), ())), preferred_element_type=F32)


def _adaln(x, g, sc, sh):
    ms = jnp.mean(x * x, axis=-1, keepdims=True)
    y = (x * lax.rsqrt(ms + EPS)) * g
    return y * (1.0 + sc) + sh


def _mod_kernel(c_ref, w_ref, b_ref, o_ref):
    c = c_ref[...]
    s = (c * jax.nn.sigmoid(c)).astype(BF16)
    o_ref[...] = _dot(s, w_ref[...].astype(BF16)) + b_ref[...]


def _modulation(cpad, w_mod, b_mod):
    width = N_MOD * D_MODEL
    return pl.pallas_call(
        _mod_kernel,
        grid=(DEPTH, width // MOD_TILE),
        in_specs=[
            pl.BlockSpec((8, D_MODEL), lambda l, n: (0, 0)),
            pl.BlockSpec((None, D_MODEL, MOD_TILE), lambda l, n: (l, 0, n)),
            pl.BlockSpec((None, 1, MOD_TILE), lambda l, n: (l, 0, n)),
        ],
        out_specs=pl.BlockSpec((None, 8, MOD_TILE), lambda l, n: (l, 0, n)),
        out_shape=jax.ShapeDtypeStruct((DEPTH, 8, width), F32),
        compiler_params=_params(("arbitrary", "arbitrary")),
        name="modulation",
    )(cpad, w_mod, b_mod.reshape(DEPTH, 1, width))


def _ffn_kernel(x_ref, ng_ref, sh_ref, sc_ref, gt_ref, w1_ref, w3_ref, w2_ref,
                w13t_ref, w2t_ref, o_ref, h_ref):
    @pl.when(pl.program_id(1) == 0)
    def _():
        x = x_ref[...]
        h_ref[...] = _adaln(x, ng_ref[...], sc_ref[...], sh_ref[...]).astype(BF16)
        o_ref[...] = x

    def down(a, b, w2):
        g = (a * jax.nn.sigmoid(a) * b).astype(BF16)
        o_ref[...] += (0.5 * gt_ref[...]) * _dot(g, w2)

    h = h_ref[...]
    down(_dot(h, w1_ref[...]), _dot(h, w3_ref[...]), w2_ref[...])

    @pl.when(pl.program_id(1) == N_FF_TILES - 1)
    def _():
        ab = _dot(h_ref[...], w13t_ref[...])
        down(ab[:, :FF_TAIL], ab[:, FF_TAIL:], w2t_ref[...])


def _ffn(x, ng, sh, sc, gt, w1, w3, w2, w13t, w2t, l, s):
    step = TOK_TILE // GROUP
    mod_spec = pl.BlockSpec((None, 1, D_MODEL), lambda i, j: (i * step, 0, 0))
    return pl.pallas_call(
        _ffn_kernel,
        grid=(N_TOK // TOK_TILE, N_FF_TILES),
        in_specs=[
            pl.BlockSpec((TOK_TILE, D_MODEL), lambda i, j: (i, 0)),
            pl.BlockSpec((1, D_MODEL), lambda i, j: (0, 0)),
            mod_spec, mod_spec, mod_spec,
            pl.BlockSpec((None, None, D_MODEL, FF_TILE), lambda i, j: (l, s, 0, j)),
            pl.BlockSpec((None, None, D_MODEL, FF_TILE), lambda i, j: (l, s, 0, j)),
            pl.BlockSpec((None, None, FF_TILE, D_MODEL), lambda i, j: (l, s, j, 0)),
            pl.BlockSpec((None, None, D_MODEL, 2 * FF_TAIL), lambda i, j: (l, s, 0, 0),
                         pipeline_mode=pl.Buffered(1)),
            pl.BlockSpec((None, None, FF_TAIL, D_MODEL), lambda i, j: (l, s, 0, 0),
                         pipeline_mode=pl.Buffered(1)),
        ],
        out_specs=pl.BlockSpec((TOK_TILE, D_MODEL), lambda i, j: (i, 0)),
        out_shape=jax.ShapeDtypeStruct((N_TOK, D_MODEL), F32),
        scratch_shapes=[pltpu.VMEM((TOK_TILE, D_MODEL), BF16)],
        compiler_params=_params(("arbitrary", "arbitrary"), BIG_VMEM_LIMIT),
        name="ffn",
    )(x, ng, sh, sc, gt, w1, w3, w2, w13t, w2t)


def _inproj_kernel(x_ref, ng_ref, sh_ref, sc_ref, w_ref, o_ref, h_ref):
    @pl.when(pl.program_id(1) == 0)
    def _():
        h_ref[...] = _adaln(x_ref[...], ng_ref[...], sc_ref[...], sh_ref[...]).astype(BF16)

    o_ref[...] = _dot(h_ref[...], w_ref[...])


def _inproj(x, ng, sh, sc, w_in, l):
    step = IN_TOK_TILE // GROUP
    mod_spec = pl.BlockSpec((None, 1, D_MODEL), lambda i, j: (i * step, 0, 0))
    return pl.pallas_call(
        _inproj_kernel,
        grid=(N_TOK // IN_TOK_TILE, IN_WIDTH // IN_TILE),
        in_specs=[
            pl.BlockSpec((IN_TOK_TILE, D_MODEL), lambda i, j: (i, 0)),
            pl.BlockSpec((1, D_MODEL), lambda i, j: (0, 0)),
            mod_spec, mod_spec,
            pl.BlockSpec((None, D_MODEL, IN_TILE), lambda i, j: (l, 0, j)),
        ],
        out_specs=pl.BlockSpec((IN_TOK_TILE, IN_TILE), lambda i, j: (i, j)),
        out_shape=jax.ShapeDtypeStruct((N_TOK, IN_WIDTH), F32),
        scratch_shapes=[pltpu.VMEM((IN_TOK_TILE, D_MODEL), BF16)],
        compiler_params=_params(("arbitrary", "arbitrary"), BIG_VMEM_LIMIT),
        name="inproj",
    )(x, ng, sh, sc, w_in)


def _pair_rmsnorm(x, g, first):
    x2 = x * x
    s0 = jnp.sum(jnp.where(first, x2, 0.0), axis=-1, keepdims=True)
    s1 = jnp.sum(jnp.where(first, 0.0, x2), axis=-1, keepdims=True)
    inv = 1.0 / HEAD_DIM
    r = jnp.where(first, lax.rsqrt(s0 * inv + EPS), lax.rsqrt(s1 * inv + EPS))
    return (x * r) * g


def _first_head_mask():
    return lax.broadcasted_iota(jnp.int32, (1, LANES), 1) < HEAD_DIM


def _ctx_attn_kernel(q_ref, k_ref, v_ref, qg_ref, kg_ref, *refs):
    o_ref, nk_ref, nv_ref = refs[-3:]
    first = _first_head_mask()
    for p in range(HEAD_PAIRS):
        cols = slice(p * LANES, (p + 1) * LANES)
        qn = _pair_rmsnorm(q_ref[:, cols], qg_ref[...], first) * (HEAD_DIM ** -0.5)
        kn = _pair_rmsnorm(k_ref[:, cols], kg_ref[...], first)
        v = v_ref[:, cols]
        nk_ref[:, cols] = kn
        nv_ref[:, cols] = v
        kb = kn.astype(BF16)
        vb = v.astype(BF16)
        outs = []
        for hh in range(2):
            sel = first if hh == 0 else jnp.logical_not(first)
            qm = jnp.where(sel, qn, 0.0).astype(BF16)
            s = _dot_nt(qm, kb)
            m = jnp.max(s, axis=-1, keepdims=True)
            e = jnp.exp(s - m)
            den = jnp.sum(e, axis=-1, keepdims=True)
            outs.append(_dot(e.astype(BF16), vb) / den)
        o_ref[:, cols] = jnp.where(first, outs[0], outs[1]).astype(BF16)


def _ctx_attention(u, qg, kg, new_k, new_v, l):
    qkv = lambda c: pl.BlockSpec((SEQ, D_ATTN), lambda b: (b, c))
    cache_spec = pl.BlockSpec((None, None, SEQ, D_ATTN), lambda b: (b, l, 0, 0))
    gain_spec = pl.BlockSpec((1, LANES), lambda b: (0, 0))
    any_spec = pl.BlockSpec(memory_space=pl.ANY)
    caches = () if new_k is None else (new_k, new_v)
    cache_shape = jax.ShapeDtypeStruct((BATCH, DEPTH, SEQ, D_ATTN), F32)
    return pl.pallas_call(
        _ctx_attn_kernel,
        grid=(BATCH,),
        in_specs=[qkv(COL_Q // D_ATTN), qkv(COL_K // D_ATTN), qkv(COL_V // D_ATTN),
                  gain_spec, gain_spec] + [any_spec] * len(caches),
        out_specs=[pl.BlockSpec((SEQ, D_ATTN), lambda b: (b, 0)), cache_spec, cache_spec],
        out_shape=[jax.ShapeDtypeStruct((N_TOK, D_ATTN), BF16), cache_shape, cache_shape],
        input_output_aliases={5: 1, 6: 2} if caches else {},
        compiler_params=_params(("arbitrary",)),
        name="ctx_attention",
    )(u, u, u, qg, kg, *caches)


def _window_start(r):
    return min(max(r - WIN_H // 2, 0), GRID_ROWS - WIN_H)


def _key_window_row(blk):
    return min(max(_window_start(blk * Q_ROWS), 0), GRID_ROWS - KEY_ROWS)


def _bias_kernel(rpb_ref, o_ref, t_ref):
    h = pl.program_id(0)
    n_dc = 2 * WIN_W - 1
    n_dr = 2 * WIN_H - 1
    lane = lax.broadcasted_iota(jnp.int32, (GRID_W, LANES), 1)
    qc = lax.broadcasted_iota(jnp.int32, (GRID_W, LANES), 0)
    kc = lane & (GRID_W - 1)
    dc = kc - qc
    col_start = jnp.clip(qc - WIN_W // 2, 0, GRID_W - WIN_W)
    col_in = (kc >= col_start) & (kc < col_start + WIN_W)
    base = h * (n_dr * n_dc)
    for dr in range(n_dr):
        row = base + dr * n_dc
        t = jnp.full((GRID_W, LANES), rpb_ref[row], F32)
        for j in range(1, n_dc - 1):
            t = jnp.where(dc == j - (WIN_W - 1), rpb_ref[row + j], t)
        t = jnp.where(dc >= WIN_W - 1, rpb_ref[row + n_dc - 1], t)
        t_ref[dr] = jnp.where(col_in, t, NEG)

    left = lane < GRID_W
    neg = jnp.full((GRID_W, LANES), NEG, F32)
    for blk in range(N_QBLK):
        ws = _key_window_row(blk)
        for qi in range(Q_ROWS):
            qr = blk * Q_ROWS + qi
            st = _window_start(qr)

            def tile(kj):
                kr = ws + kj
                if st <= kr < st + WIN_H:
                    return t_ref[kr - qr + WIN_H - 1]
                return None

            for g in range(KEY_ROWS // 2):
                a, b = tile(2 * g), tile(2 * g + 1)
                if a is None and b is None:
                    val = neg
                else:
                    val = jnp.where(left, neg if a is None else a, neg if b is None else b)
                o_ref[blk, qi * GRID_W:(qi + 1) * GRID_W, g * LANES:(g + 1) * LANES] = val


def _bias_table(rpb_l):
    n = (2 * WIN_H - 1) * (2 * WIN_W - 1)
    return pl.pallas_call(
        _bias_kernel,
        grid=(N_HEADS,),
        in_specs=[pl.BlockSpec(memory_space=pltpu.SMEM)],
        out_specs=pl.BlockSpec((None, N_QBLK, Q_BLK, KEY_BLK), lambda h: (h, 0, 0, 0)),
        out_shape=jax.ShapeDtypeStruct((N_HEADS, N_QBLK, Q_BLK, KEY_BLK), F32),
        scratch_shapes=[pltpu.VMEM((2 * WIN_H - 1, GRID_W, LANES), F32)],
        compiler_params=_params(("arbitrary",)),
        name="bias_table",
    )(rpb_l.reshape(N_HEADS * n))


def _lat_attn_kernel(q_ref, k_ref, v_ref, ck_ref, cv_ref, bias_ref, qg_ref, kg_ref, o_in, o_ref):
    del o_in
    first = _first_head_mask()
    second = jnp.logical_not(first)
    qn = _pair_rmsnorm(q_ref[...], qg_ref[...], first) * (HEAD_DIM ** -0.5)
    kb = _pair_rmsnorm(k_ref[...], kg_ref[...], first).astype(BF16)
    vb = v_ref[...].astype(BF16)
    ckb = ck_ref[...].astype(BF16)
    cvb = cv_ref[...].astype(BF16)
    for blk in range(N_QBLK):
        rows = slice(blk * Q_BLK, (blk + 1) * Q_BLK)
        ws = _key_window_row(blk) * GRID_W
        kw = kb[ws:ws + KEY_BLK]
        vw = vb[ws:ws + KEY_BLK]
        outs = []
        for hh, sel in enumerate((first, second)):
            qm = jnp.where(sel, qn[rows], 0.0).astype(BF16)
            s_loc = _dot_nt(qm, kw) + bias_ref[hh, blk]
            s_ctx = _dot_nt(qm, ckb)
            m = jnp.maximum(jnp.max(s_loc, axis=-1, keepdims=True), jnp.max(s_ctx, axis=-1, keepdims=True))
            e_loc = jnp.exp(s_loc - m)
            e_ctx = jnp.exp(s_ctx - m)
            den = jnp.sum(e_loc, axis=-1, keepdims=True) + jnp.sum(e_ctx, axis=-1, keepdims=True)
            acc = _dot(e_loc.astype(BF16), vw) + _dot(e_ctx.astype(BF16), cvb)
            outs.append(acc / den)
        o_ref[rows, :] = jnp.where(first, outs[0], outs[1]).astype(BF16)


def _lat_attention(u, cache_k, cache_v, bias, qg, kg, attn, l):
    seq0 = N_CTX // DEC_SEQ
    gain_spec = pl.BlockSpec((1, LANES), lambda b, p: (0, 0))
    cache_spec = pl.BlockSpec((None, None, PAST_LEN, LANES), lambda b, p: (b, l, 0, p))
    qkv = lambda col: pl.BlockSpec((DEC_SEQ, LANES), lambda b, p: (seq0 + b, col // LANES + p))
    return pl.pallas_call(
        _lat_attn_kernel,
        grid=(DEC_BATCH, HEAD_PAIRS),
        in_specs=[
            qkv(COL_Q), qkv(COL_K), qkv(COL_V),
            cache_spec, cache_spec,
            pl.BlockSpec((2, N_QBLK, Q_BLK, KEY_BLK), lambda b, p: (p, 0, 0, 0)),
            gain_spec, gain_spec,
            pl.BlockSpec(memory_space=pl.ANY),
        ],
        out_specs=pl.BlockSpec((DEC_SEQ, LANES), lambda b, p: (seq0 + b, p)),
        out_shape=jax.ShapeDtypeStruct((N_TOK, D_ATTN), BF16),
        input_output_aliases={8: 0},
        compiler_params=_params(("arbitrary", "arbitrary")),
        name="lat_attention",
    )(u, u, u, cache_k, cache_v, bias, qg, kg, attn)


def _halo_masks(i):
    is_lat = i >= CTX_GROUPS
    j = i & (LAT_GROUPS_PER_SEQ - 1)
    return is_lat & (j != 0), is_lat & (j != LAT_GROUPS_PER_SEQ - 1)


def _conv_kernel(cur_ref, prev_ref, next_ref, dw_ref, cb_ref, cg_ref, o_ref, hp_ref, acc_ref, sh_ref):
    has_prev, has_next = _halo_masks(pl.program_id(0))

    def glu(z):
        return z[:, :D_CONV] * jax.nn.sigmoid(z[:, D_CONV:])

    hp_ref[0:HALO] = jnp.where(has_prev, glu(prev_ref[...]), 0.0)
    hp_ref[HALO:HALO + GROUP] = glu(cur_ref[...])
    hp_ref[HALO + GROUP:] = jnp.where(has_next, glu(next_ref[...]), 0.0)

    rows = 64
    off = HALO - CONV_WIDTH // 2
    span = GROUP + 2 * HALO - SUBLANES
    for lc in range(D_CONV // LANES):
        cols = slice(lc * LANES, (lc + 1) * LANES)
        for r in range(SUBLANES):
            sh_ref[r, 0:span] = hp_ref[pl.ds(r, span), cols]
        accs = [jnp.zeros((rows, LANES), F32) for _ in range(GROUP // rows)]
        for j in range(CONV_WIDTH):
            r = (off + j) % SUBLANES
            w = dw_ref[j:j + 1, cols]
            for rc in range(GROUP // rows):
                base = rc * rows + (off + j) - r
                accs[rc] = accs[rc] + sh_ref[r, base:base + rows] * w
        for rc in range(GROUP // rows):
            acc_ref[rc * rows:(rc + 1) * rows, cols] = accs[rc] + cb_ref[:, cols]

    y = acc_ref[...]
    ms = jnp.mean(y * y, axis=-1, keepdims=True)
    y = (y * lax.rsqrt(ms + EPS)) * cg_ref[...]
    o_ref[...] = (y * jax.nn.sigmoid(y)).astype(BF16)


def _halo_specs(width, col):
    blocks_per_group = GROUP // HALO
    last = N_TOK // HALO - 1
    cur = pl.BlockSpec((GROUP, width), lambda i: (i, col))
    prev = pl.BlockSpec((HALO, width), lambda i: (jnp.maximum(i * blocks_per_group - 1, 0), col))
    nxt = pl.BlockSpec((HALO, width), lambda i: (jnp.minimum((i + 1) * blocks_per_group, last), col))
    return [cur, prev, nxt]


def _conv_mixer(u, dw, cb, cg):
    full = lambda shape: pl.BlockSpec(shape, lambda i: (0, 0))
    return pl.pallas_call(
        _conv_kernel,
        grid=(N_GROUPS,),
        in_specs=_halo_specs(2 * D_CONV, COL_CONV // (2 * D_CONV))
        + [full((CONV_WIDTH, D_CONV)), full((1, D_CONV)), full((1, D_CONV))],
        out_specs=pl.BlockSpec((GROUP, D_CONV), lambda i: (i, 0)),
        out_shape=jax.ShapeDtypeStruct((N_TOK, D_CONV), BF16),
        scratch_shapes=[pltpu.VMEM((GROUP + 2 * HALO, D_CONV), F32), pltpu.VMEM((GROUP, D_CONV), F32),
                        pltpu.VMEM((SUBLANES, GROUP + 2 * HALO, LANES), F32)],
        compiler_params=_params(("arbitrary",)),
        name="conv_mixer",
    )(u, u, u, dw, cb, cg)


def _pool_kernel(cur_ref, prev_ref, next_ref, pw_ref, ps_ref, o_ref, zp_ref):
    i = pl.program_id(0)
    has_prev, has_next = _halo_masks(i)
    zp_ref[0:HALO] = jnp.where(has_prev, prev_ref[...], 0.0)
    zp_ref[HALO:HALO + GROUP] = cur_ref[...]
    zp_ref[HALO + GROUP:] = jnp.where(has_next, next_ref[...], 0.0)

    is_lat = i >= CTX_GROUPS
    seq_len = jnp.where(is_lat, DEC_SEQ, SEQ)
    pos0 = jnp.where(is_lat, (i & (LAT_GROUPS_PER_SEQ - 1)) * GROUP, 0)
    pos = lax.broadcasted_iota(jnp.int32, (GROUP, 1), 0) + pos0
    for gi, w in enumerate(POOL_WINDOWS):
        cols = slice(gi * POOL_GROUP, (gi + 1) * POOL_GROUP)
        tot = zp_ref[pl.ds(HALO - w // 2, GROUP), cols]
        for d in range(1, w):
            tot = tot + zp_ref[pl.ds(HALO - w // 2 + d, GROUP), cols]
        lo = jnp.clip(pos - w // 2, 0, seq_len)
        hi = jnp.clip(pos - w // 2 + w, 0, seq_len)
        mean = tot / (hi - lo).astype(F32)
        d = (mean - cur_ref[:, cols]).astype(BF16)
        o_ref[:, cols] = (_dot(d, pw_ref[gi]) * ps_ref[:, cols]).astype(BF16)


def _pool_mixer(u, pw, ps):
    return pl.pallas_call(
        _pool_kernel,
        grid=(N_GROUPS,),
        in_specs=_halo_specs(D_POOL, COL_POOL // D_POOL)
        + [pl.BlockSpec((len(POOL_WINDOWS), POOL_GROUP, POOL_GROUP), lambda i: (0, 0, 0)),
           pl.BlockSpec((1, D_POOL), lambda i: (0, 0))],
        out_specs=pl.BlockSpec((GROUP, D_POOL), lambda i: (i, 0)),
        out_shape=jax.ShapeDtypeStruct((N_TOK, D_POOL), BF16),
        scratch_shapes=[pltpu.VMEM((GROUP + 2 * HALO, D_POOL), F32)],
        compiler_params=_params(("arbitrary",)),
        name="pool_mixer",
    )(u, u, u, pw, ps)


def _merge_kernel(x_ref, a_ref, b_ref, c_ref, g0_ref, g1_ref, g2_ref, g3_ref, gt_ref,
                  wa_ref, wb_ref, wc_ref, wo_ref, o_ref):
    gate_refs = (g0_ref, g1_ref, g2_ref, g3_ref)

    def gate(k):
        parts = []
        lo, hi = k * D_MODEL, (k + 1) * D_MODEL
        while lo < hi:
            blk, start = divmod(lo, GATE_BLK)
            stop = min(GATE_BLK, start + hi - lo)
            parts.append(gate_refs[blk][:, start:stop])
            lo += stop - start
        return jax.nn.sigmoid(jnp.concatenate(parts, axis=1))

    merged = gate(0) * _dot(a_ref[...], wa_ref[...])
    merged += gate(1) * _dot(b_ref[...], wb_ref[...])
    merged += gate(2) * _dot(c_ref[...], wc_ref[...])
    o_ref[...] = x_ref[...] + gt_ref[...] * _dot(merged.astype(BF16), wo_ref[...])


def _merge(x, attn, conv, pool, u, gt, wa, wb, wc, wo, l):
    tok = lambda width, col=0: pl.BlockSpec((MERGE_TILE, width), lambda i: (i, col))
    gate0 = COL_GATE // GATE_BLK
    weight = lambda rows: pl.BlockSpec((None, rows, D_MODEL), lambda i: (l, 0, 0),
                                       pipeline_mode=pl.Buffered(1))
    return pl.pallas_call(
        _merge_kernel,
        grid=(N_TOK // MERGE_TILE,),
        in_specs=[
            tok(D_MODEL), tok(D_ATTN), tok(D_CONV), tok(D_POOL),
            tok(GATE_BLK, gate0), tok(GATE_BLK, gate0 + 1), tok(GATE_BLK, gate0 + 2), tok(GATE_BLK, gate0 + 3),
            pl.BlockSpec((None, 1, D_MODEL), lambda i: (i * (MERGE_TILE // GROUP), 0, 0)),
            weight(D_ATTN), weight(D_CONV), weight(D_POOL), weight(D_MODEL),
        ],
        out_specs=tok(D_MODEL),
        out_shape=jax.ShapeDtypeStruct((N_TOK, D_MODEL), F32),
        compiler_params=_params(("arbitrary",)),
        name="merge",
    )(x, attn, conv, pool, u, u, u, u, gt, wa, wb, wc, wo)


def kernel(x_prompt, x_sample, cache_k, cache_v, c, c_ctx, w_mod, b_mod, norm_g, ffn_w1, ffn_w3, ffn_w2,
           w_in, qk_g, rpb, w_br_a, conv_dw, conv_b, conv_g, w_br_b, pool_w, pool_scale, w_br_c, w_out):
    x = jnp.concatenate([x_prompt.reshape(N_CTX, D_MODEL), x_sample.reshape(N_LAT, D_MODEL)], axis=0)

    cpad = jnp.concatenate([c_ctx[None, :], c, jnp.zeros((8 - 1 - DEC_BATCH, D_MODEL), F32)], axis=0)
    mod = _modulation(cpad, w_mod, b_mod)
    group_row = jnp.concatenate([jnp.zeros((CTX_GROUPS,), jnp.int32),
                                 1 + jnp.arange(N_LAT // GROUP, dtype=jnp.int32) // LAT_GROUPS_PER_SEQ])
    mod = mod[:, group_row].reshape(DEPTH, N_GROUPS, N_MOD, 1, D_MODEL).transpose(0, 2, 1, 3, 4)

    w1_b, w3_b, w2_b, w_in_b = (w.astype(BF16) for w in (ffn_w1, ffn_w3, ffn_w2, w_in))
    tail0 = N_FF_TILES * FF_TILE
    w13t = jnp.concatenate([w1_b[..., tail0:], w3_b[..., tail0:]], axis=-1)
    w2t = w2_b[:, :, tail0:, :]
    wa_b, wb_b, wc_b, wo_b = (w.astype(BF16) for w in (w_br_a, w_br_b, w_br_c, w_out))
    pool_w_b = pool_w.astype(BF16)
    qk_pair = jnp.tile(qk_g, (1, 1, 2)).reshape(DEPTH, 2, 1, LANES)
    ck = cache_k.reshape(DEC_BATCH, DEPTH, PAST_LEN, D_ATTN)
    cv = cache_v.reshape(DEC_BATCH, DEPTH, PAST_LEN, D_ATTN)

    new_k = new_v = None

    for l in range(DEPTH):
        m = mod[l]
        ng = norm_g[l][:, None, :]
        ffn_w = (w1_b, w3_b, w2_b, w13t, w2t)
        x = _ffn(x, ng[0], m[0], m[1], m[2], *ffn_w, l, 0)
        u = _inproj(x, ng[1], m[3], m[4], w_in_b, l)
        attn, new_k, new_v = _ctx_attention(u, qk_pair[l, 0], qk_pair[l, 1], new_k, new_v, l)
        attn = _lat_attention(u, ck, cv, _bias_table(rpb[l]), qk_pair[l, 0], qk_pair[l, 1], attn, l)
        conv = _conv_mixer(u, conv_dw[l], conv_b[l][None, :], conv_g[l][None, :])
        pool = _pool_mixer(u, pool_w_b[l], pool_scale[l][None, :])
        x = _merge(x, attn, conv, pool, u, m[5], wa_b, wb_b, wc_b, wo_b, l)
        x = _ffn(x, ng[2], m[6], m[7], m[8], *ffn_w, l, 1)

    y_prompt = x[:N_CTX].reshape(BATCH, SEQ, D_MODEL)
    y_sample = x[N_CTX:].reshape(DEC_BATCH, DEC_SEQ, D_MODEL)
    shape = (BATCH, DEPTH, SEQ, N_HEADS, HEAD_DIM)
    return y_prompt, y_sample, new_k.reshape(shape), new_v.reshape(shape)
```

```python
import functools

import jax
import jax.numpy as jnp
from jax import lax
from jax.experimental import pallas as pl
from jax.experimental.pallas import tpu as pltpu

D_MODEL = 2048
BATCH = 32
SEQ = 256
DEPTH = 4
DEC_BATCH = 4
DEC_SEQ = 1024
PAST_LEN = 512
GRID_W = 64
GRID_ROWS = DEC_SEQ // GRID_W
D_ATTN = D_MODEL // 2
N_HEADS = 16
HEAD_DIM = D_ATTN // N_HEADS
WIN_H = 8
WIN_W = 16
D_CONV = D_MODEL // 4
CONV_WIDTH = 31
D_POOL = D_MODEL // 4
POOL_WINDOWS = (2, 4, 8, 16)
POOL_GROUP = D_POOL // len(POOL_WINDOWS)
D_FF = ((8 * D_MODEL // 3 + 127) // 128) * 128
N_MOD = 9
EPS = 1e-6
IN_WIDTH = 3 * D_ATTN + 2 * D_CONV + D_POOL + 3 * D_MODEL

N_CTX = BATCH * SEQ
N_LAT = DEC_BATCH * DEC_SEQ
N_TOK = N_CTX + N_LAT
GROUP = 256
N_GROUPS = N_TOK // GROUP
CTX_GROUPS = N_CTX // GROUP
LAT_GROUPS_PER_SEQ = DEC_SEQ // GROUP

LANES = 128
SUBLANES = 8
HEAD_PAIRS = N_HEADS // 2
HALO = 16

COL_Q = 0
COL_K = D_ATTN
COL_V = 2 * D_ATTN
COL_CONV = 3 * D_ATTN
COL_POOL = COL_CONV + 2 * D_CONV
COL_GATE = COL_POOL + D_POOL

FF_TILE = 768
N_FF_TILES = D_FF // FF_TILE
FF_TAIL = D_FF - N_FF_TILES * FF_TILE
TOK_TILE = 512
PREP_ROWS = TOK_TILE // (N_FF_TILES + 1)
IN_TOK_TILE = 1024
IN_TILE = 1536
MERGE_TILE = 256
GATE_BLK = 1536
MOD_TILE = 1024

Q_ROWS = 4
Q_BLK = Q_ROWS * GRID_W
N_QBLK = GRID_ROWS // Q_ROWS
KEY_ROWS = 12
KEY_BLK = KEY_ROWS * GRID_W
NEG = -1e30

VMEM_LIMIT = 48 * 1024 * 1024
BIG_VMEM_LIMIT = 56 * 1024 * 1024

F32 = jnp.float32
BF16 = jnp.bfloat16


def _params(sem, vmem=VMEM_LIMIT):
    return pltpu.CompilerParams(dimension_semantics=sem, vmem_limit_bytes=vmem)


def _dot(a, b):
    return jnp.dot(a, b, preferred_element_type=F32)


def _dot_nt(a, b):
    return lax.dot_general(a, b, (((1,), (1,)), ((), ())), preferred_element_type=F32)


def _adaln(x, g, sc, sh):
    ms = jnp.mean(x * x, axis=-1, keepdims=True)
    y = (x * lax.rsqrt(ms + EPS)) * g
    return y * (1.0 + sc) + sh


def _mod_kernel(c_ref, w_ref, b_ref, o_ref):
    c = c_ref[...]
    s = (c * jax.nn.sigmoid(c)).astype(BF16)
    o_ref[...] = _dot(s, w_ref[...].astype(BF16)) + b_ref[...]


def _modulation(cpad, w_mod, b_mod):
    width = N_MOD * D_MODEL
    return pl.pallas_call(
        _mod_kernel,
        grid=(DEPTH, width // MOD_TILE),
        in_specs=[
            pl.BlockSpec((8, D_MODEL), lambda l, n: (0, 0)),
            pl.BlockSpec((None, D_MODEL, MOD_TILE), lambda l, n: (l, 0, n)),
            pl.BlockSpec((None, 1, MOD_TILE), lambda l, n: (l, 0, n)),
        ],
        out_specs=pl.BlockSpec((None, 8, MOD_TILE), lambda l, n: (l, 0, n)),
        out_shape=jax.ShapeDtypeStruct((DEPTH, 8, width), F32),
        compiler_params=_params(("arbitrary", "arbitrary")),
        name="modulation",
    )(cpad, w_mod, b_mod.reshape(DEPTH, 1, width))


def _ffn_kernel(x0_ref, xn_ref, ng_ref, sh_ref, sc_ref, gt_ref, shn_ref, scn_ref,
                w1_ref, w3_ref, w2_ref, w13t_ref, w2t_ref, o_ref, h_ref, xhold_ref):
    i = pl.program_id(0)
    j = pl.program_id(1)
    slot = i & 1

    @pl.when((j == 0) & (i == 0))
    def _():
        x = x0_ref[...]
        h_ref[0] = _adaln(x, ng_ref[...], sc_ref[...], sh_ref[...]).astype(BF16)
        o_ref[...] = x

    @pl.when((j == 0) & (i > 0))
    def _():
        o_ref[...] = xhold_ref[...]

    def prepare_next(rows):
        xn = xn_ref[rows, :]
        h_ref[1 - slot, rows, :] = _adaln(xn, ng_ref[...], scn_ref[...], shn_ref[...]).astype(BF16)
        xhold_ref[rows, :] = xn

    def down(a, b, w2):
        g = (a * jax.nn.sigmoid(a) * b).astype(BF16)
        o_ref[...] += (0.5 * gt_ref[...]) * _dot(g, w2)

    prepare_next(pl.ds(pl.multiple_of(j * PREP_ROWS, PREP_ROWS), PREP_ROWS))
    h = h_ref[slot]
    down(_dot(h, w1_ref[...]), _dot(h, w3_ref[...]), w2_ref[...])

    @pl.when(j == N_FF_TILES - 1)
    def _():
        prepare_next(pl.ds(N_FF_TILES * PREP_ROWS, PREP_ROWS))
        ab = _dot(h_ref[slot], w13t_ref[...])
        down(ab[:, :FF_TAIL], ab[:, FF_TAIL:], w2t_ref[...])


def _ffn(x, ng, sh, sc, gt, w1, w3, w2, w13t, w2t, l, s):
    assert PREP_ROWS * (N_FF_TILES + 1) == TOK_TILE
    step = TOK_TILE // GROUP
    last = N_TOK // TOK_TILE - 1
    mod_spec = pl.BlockSpec((None, 1, D_MODEL), lambda i, j: (i * step, 0, 0))
    next_mod_spec = pl.BlockSpec((None, 1, D_MODEL), lambda i, j: (jnp.minimum(i + 1, last) * step, 0, 0))
    return pl.pallas_call(
        _ffn_kernel,
        grid=(N_TOK // TOK_TILE, N_FF_TILES),
        in_specs=[
            pl.BlockSpec((TOK_TILE, D_MODEL), lambda i, j: (0, 0), pipeline_mode=pl.Buffered(1)),
            pl.BlockSpec((TOK_TILE, D_MODEL), lambda i, j: (jnp.minimum(i + 1, last), 0)),
            pl.BlockSpec((1, D_MODEL), lambda i, j: (0, 0)),
            mod_spec, mod_spec, mod_spec, next_mod_spec, next_mod_spec,
            pl.BlockSpec((None, None, D_MODEL, FF_TILE), lambda i, j: (l, s, 0, j)),
            pl.BlockSpec((None, None, D_MODEL, FF_TILE), lambda i, j: (l, s, 0, j)),
            pl.BlockSpec((None, None, FF_TILE, D_MODEL), lambda i, j: (l, s, j, 0)),
            pl.BlockSpec((None, None, D_MODEL, 2 * FF_TAIL), lambda i, j: (l, s, 0, 0),
                         pipeline_mode=pl.Buffered(1)),
            pl.BlockSpec((None, None, FF_TAIL, D_MODEL), lambda i, j: (l, s, 0, 0),
                         pipeline_mode=pl.Buffered(1)),
        ],
        out_specs=pl.BlockSpec((TOK_TILE, D_MODEL), lambda i, j: (i, 0)),
        out_shape=jax.ShapeDtypeStruct((N_TOK, D_MODEL), F32),
        scratch_shapes=[pltpu.VMEM((2, TOK_TILE, D_MODEL), BF16), pltpu.VMEM((TOK_TILE, D_MODEL), F32)],
        compiler_params=_params(("arbitrary", "arbitrary"), BIG_VMEM_LIMIT),
        name="ffn",
    )(x, x, ng, sh, sc, gt, sh, sc, w1, w3, w2, w13t, w2t)


def _inproj_kernel(x_ref, ng_ref, sh_ref, sc_ref, w_ref, o_ref, h_ref):
    @pl.when(pl.program_id(1) == 0)
    def _():
        h_ref[...] = _adaln(x_ref[...], ng_ref[...], sc_ref[...], sh_ref[...]).astype(BF16)

    o_ref[...] = _dot(h_ref[...], w_ref[...])


def _inproj(x, ng, sh, sc, w_in, l):
    step = IN_TOK_TILE // GROUP
    mod_spec = pl.BlockSpec((None, 1, D_MODEL), lambda i, j: (i * step, 0, 0))
    return pl.pallas_call(
        _inproj_kernel,
        grid=(N_TOK // IN_TOK_TILE, IN_WIDTH // IN_TILE),
        in_specs=[
            pl.BlockSpec((IN_TOK_TILE, D_MODEL), lambda i, j: (i, 0)),
            pl.BlockSpec((1, D_MODEL), lambda i, j: (0, 0)),
            mod_spec, mod_spec,
            pl.BlockSpec((None, D_MODEL, IN_TILE), lambda i, j: (l, 0, j)),
        ],
        out_specs=pl.BlockSpec((IN_TOK_TILE, IN_TILE), lambda i, j: (i, j)),
        out_shape=jax.ShapeDtypeStruct((N_TOK, IN_WIDTH), F32),
        scratch_shapes=[pltpu.VMEM((IN_TOK_TILE, D_MODEL), BF16)],
        compiler_params=_params(("arbitrary", "arbitrary"), BIG_VMEM_LIMIT),
        name="inproj",
    )(x, ng, sh, sc, w_in)


def _pair_rmsnorm(x, g, first):
    x2 = x * x
    s0 = jnp.sum(jnp.where(first, x2, 0.0), axis=-1, keepdims=True)
    s1 = jnp.sum(jnp.where(first, 0.0, x2), axis=-1, keepdims=True)
    inv = 1.0 / HEAD_DIM
    r = jnp.where(first, lax.rsqrt(s0 * inv + EPS), lax.rsqrt(s1 * inv + EPS))
    return (x * r) * g


def _first_head_mask():
    return lax.broadcasted_iota(jnp.int32, (1, LANES), 1) < HEAD_DIM


def _ctx_attn_kernel(q_ref, k_ref, v_ref, qg_ref, kg_ref, *refs):
    o_ref, nk_ref, nv_ref = refs[-3:]
    first = _first_head_mask()
    for p in range(HEAD_PAIRS):
        cols = slice(p * LANES, (p + 1) * LANES)
        qn = _pair_rmsnorm(q_ref[:, cols], qg_ref[...], first) * (HEAD_DIM ** -0.5)
        kn = _pair_rmsnorm(k_ref[:, cols], kg_ref[...], first)
        v = v_ref[:, cols]
        nk_ref[:, cols] = kn
        nv_ref[:, cols] = v
        kb = kn.astype(BF16)
        vb = v.astype(BF16)
        outs = []
        for hh in range(2):
            sel = first if hh == 0 else jnp.logical_not(first)
            qm = jnp.where(sel, qn, 0.0).astype(BF16)
            s = _dot_nt(qm, kb)
            m = jnp.max(s, axis=-1, keepdims=True)
            e = jnp.exp(s - m)
            den = jnp.sum(e, axis=-1, keepdims=True)
            outs.append(_dot(e.astype(BF16), vb) / den)
        o_ref[:, cols] = jnp.where(first, outs[0], outs[1]).astype(BF16)


def _ctx_attention(u, qg, kg, new_k, new_v, l):
    qkv = lambda c: pl.BlockSpec((SEQ, D_ATTN), lambda b: (b, c))
    cache_spec = pl.BlockSpec((None, None, SEQ, D_ATTN), lambda b: (b, l, 0, 0))
    gain_spec = pl.BlockSpec((1, LANES), lambda b: (0, 0))
    any_spec = pl.BlockSpec(memory_space=pl.ANY)
    caches = () if new_k is None else (new_k, new_v)
    cache_shape = jax.ShapeDtypeStruct((BATCH, DEPTH, SEQ, D_ATTN), F32)
    return pl.pallas_call(
        _ctx_attn_kernel,
        grid=(BATCH,),
        in_specs=[qkv(COL_Q // D_ATTN), qkv(COL_K // D_ATTN), qkv(COL_V // D_ATTN),
                  gain_spec, gain_spec] + [any_spec] * len(caches),
        out_specs=[pl.BlockSpec((SEQ, D_ATTN), lambda b: (b, 0)), cache_spec, cache_spec],
        out_shape=[jax.ShapeDtypeStruct((N_TOK, D_ATTN), BF16), cache_shape, cache_shape],
        input_output_aliases={5: 1, 6: 2} if caches else {},
        compiler_params=_params(("arbitrary",)),
        name="ctx_attention",
    )(u, u, u, qg, kg, *caches)


def _window_start(r):
    return min(max(r - WIN_H // 2, 0), GRID_ROWS - WIN_H)


def _key_window_row(blk):
    return min(max(_window_start(blk * Q_ROWS), 0), GRID_ROWS - KEY_ROWS)


def _bias_kernel(rpb_ref, o_ref, t_ref):
    h = pl.program_id(0)
    n_dc = 2 * WIN_W - 1
    n_dr = 2 * WIN_H - 1
    lane = lax.broadcasted_iota(jnp.int32, (GRID_W, LANES), 1)
    qc = lax.broadcasted_iota(jnp.int32, (GRID_W, LANES), 0)
    kc = lane & (GRID_W - 1)
    dc = kc - qc
    col_start = jnp.clip(qc - WIN_W // 2, 0, GRID_W - WIN_W)
    col_in = (kc >= col_start) & (kc < col_start + WIN_W)
    base = h * (n_dr * n_dc)
    for dr in range(n_dr):
        row = base + dr * n_dc
        t = jnp.full((GRID_W, LANES), rpb_ref[row], F32)
        for j in range(1, n_dc - 1):
            t = jnp.where(dc == j - (WIN_W - 1), rpb_ref[row + j], t)
        t = jnp.where(dc >= WIN_W - 1, rpb_ref[row + n_dc - 1], t)
        t_ref[dr] = jnp.where(col_in, t, NEG)

    left = lane < GRID_W
    neg = jnp.full((GRID_W, LANES), NEG, F32)
    for blk in range(N_QBLK):
        ws = _key_window_row(blk)
        for qi in range(Q_ROWS):
            qr = blk * Q_ROWS + qi
            st = _window_start(qr)

            def tile(kj):
                kr = ws + kj
                if st <= kr < st + WIN_H:
                    return t_ref[kr - qr + WIN_H - 1]
                return None

            for g in range(KEY_ROWS // 2):
                a, b = tile(2 * g), tile(2 * g + 1)
                if a is None and b is None:
                    val = neg
                else:
                    val = jnp.where(left, neg if a is None else a, neg if b is None else b)
                o_ref[blk, qi * GRID_W:(qi + 1) * GRID_W, g * LANES:(g + 1) * LANES] = val


def _bias_table(rpb_l):
    n = (2 * WIN_H - 1) * (2 * WIN_W - 1)
    return pl.pallas_call(
        _bias_kernel,
        grid=(N_HEADS,),
        in_specs=[pl.BlockSpec(memory_space=pltpu.SMEM)],
        out_specs=pl.BlockSpec((None, N_QBLK, Q_BLK, KEY_BLK), lambda h: (h, 0, 0, 0)),
        out_shape=jax.ShapeDtypeStruct((N_HEADS, N_QBLK, Q_BLK, KEY_BLK), F32),
        scratch_shapes=[pltpu.VMEM((2 * WIN_H - 1, GRID_W, LANES), F32)],
        compiler_params=_params(("arbitrary",)),
        name="bias_table",
    )(rpb_l.reshape(N_HEADS * n))


def _lat_attn_kernel(q_ref, k_ref, v_ref, ck_ref, cv_ref, bias_ref, qg_ref, kg_ref, o_in, o_ref):
    del o_in
    first = _first_head_mask()
    second = jnp.logical_not(first)
    qn = _pair_rmsnorm(q_ref[...], qg_ref[...], first) * (HEAD_DIM ** -0.5)
    kb = _pair_rmsnorm(k_ref[...], kg_ref[...], first).astype(BF16)
    vb = v_ref[...].astype(BF16)
    ckb = ck_ref[...].astype(BF16)
    cvb = cv_ref[...].astype(BF16)
    for blk in range(N_QBLK):
        rows = slice(blk * Q_BLK, (blk + 1) * Q_BLK)
        ws = _key_window_row(blk) * GRID_W
        kw = kb[ws:ws + KEY_BLK]
        vw = vb[ws:ws + KEY_BLK]
        outs = []
        for hh, sel in enumerate((first, second)):
            qm = jnp.where(sel, qn[rows], 0.0).astype(BF16)
            s_loc = _dot_nt(qm, kw) + bias_ref[hh, blk]
            s_ctx = _dot_nt(qm, ckb)
            m = jnp.maximum(jnp.max(s_loc, axis=-1, keepdims=True), jnp.max(s_ctx, axis=-1, keepdims=True))
            e_loc = jnp.exp(s_loc - m)
            e_ctx = jnp.exp(s_ctx - m)
            den = jnp.sum(e_loc, axis=-1, keepdims=True) + jnp.sum(e_ctx, axis=-1, keepdims=True)
            acc = _dot(e_loc.astype(BF16), vw) + _dot(e_ctx.astype(BF16), cvb)
            outs.append(acc / den)
        o_ref[rows, :] = jnp.where(first, outs[0], outs[1]).astype(BF16)


def _lat_attention(u, cache_k, cache_v, bias, qg, kg, attn, l):
    seq0 = N_CTX // DEC_SEQ
    gain_spec = pl.BlockSpec((1, LANES), lambda b, p: (0, 0))
    cache_spec = pl.BlockSpec((None, None, PAST_LEN, LANES), lambda b, p: (b, l, 0, p))
    qkv = lambda col: pl.BlockSpec((DEC_SEQ, LANES), lambda b, p: (seq0 + b, col // LANES + p))
    return pl.pallas_call(
        _lat_attn_kernel,
        grid=(DEC_BATCH, HEAD_PAIRS),
        in_specs=[
            qkv(COL_Q), qkv(COL_K), qkv(COL_V),
            cache_spec, cache_spec,
            pl.BlockSpec((2, N_QBLK, Q_BLK, KEY_BLK), lambda b, p: (p, 0, 0, 0)),
            gain_spec, gain_spec,
            pl.BlockSpec(memory_space=pl.ANY),
        ],
        out_specs=pl.BlockSpec((DEC_SEQ, LANES), lambda b, p: (seq0 + b, p)),
        out_shape=jax.ShapeDtypeStruct((N_TOK, D_ATTN), BF16),
        input_output_aliases={8: 0},
        compiler_params=_params(("arbitrary", "arbitrary")),
        name="lat_attention",
    )(u, u, u, cache_k, cache_v, bias, qg, kg, attn)


def _halo_masks(i):
    is_lat = i >= CTX_GROUPS
    j = i & (LAT_GROUPS_PER_SEQ - 1)
    return is_lat & (j != 0), is_lat & (j != LAT_GROUPS_PER_SEQ - 1)


def _conv_block(i, cur_ref, prev_ref, next_ref, dw_ref, cb_ref, cg_ref, hp_ref, acc_ref, sh_ref):
    has_prev, has_next = _halo_masks(i)

    def glu(z):
        return z[:, :D_CONV] * jax.nn.sigmoid(z[:, D_CONV:])

    hp_ref[0:HALO] = jnp.where(has_prev, glu(prev_ref[...]), 0.0)
    hp_ref[HALO:HALO + GROUP] = glu(cur_ref[...])
    hp_ref[HALO + GROUP:] = jnp.where(has_next, glu(next_ref[...]), 0.0)

    rows = 64
    off = HALO - CONV_WIDTH // 2
    span = GROUP + 2 * HALO - SUBLANES
    for lc in range(D_CONV // LANES):
        cols = slice(lc * LANES, (lc + 1) * LANES)
        for r in range(SUBLANES):
            sh_ref[r, 0:span] = hp_ref[pl.ds(r, span), cols]
        accs = [jnp.zeros((rows, LANES), F32) for _ in range(GROUP // rows)]
        for j in range(CONV_WIDTH):
            r = (off + j) % SUBLANES
            w = dw_ref[j:j + 1, cols]
            for rc in range(GROUP // rows):
                base = rc * rows + (off + j) - r
                accs[rc] = accs[rc] + sh_ref[r, base:base + rows] * w
        for rc in range(GROUP // rows):
            acc_ref[rc * rows:(rc + 1) * rows, cols] = accs[rc] + cb_ref[:, cols]

    y = acc_ref[...]
    ms = jnp.mean(y * y, axis=-1, keepdims=True)
    y = (y * lax.rsqrt(ms + EPS)) * cg_ref[...]
    return (y * jax.nn.sigmoid(y)).astype(BF16)


def _halo_specs(width, col):
    blocks_per_group = GROUP // HALO
    last = N_TOK // HALO - 1
    cur = pl.BlockSpec((GROUP, width), lambda i: (i, col))
    prev = pl.BlockSpec((HALO, width), lambda i: (jnp.maximum(i * blocks_per_group - 1, 0), col))
    nxt = pl.BlockSpec((HALO, width), lambda i: (jnp.minimum((i + 1) * blocks_per_group, last), col))
    return [cur, prev, nxt]


def _pool_block(i, cur_ref, prev_ref, next_ref, pw_ref, ps_ref, zp_ref):
    has_prev, has_next = _halo_masks(i)
    zp_ref[0:HALO] = jnp.where(has_prev, prev_ref[...], 0.0)
    zp_ref[HALO:HALO + GROUP] = cur_ref[...]
    zp_ref[HALO + GROUP:] = jnp.where(has_next, next_ref[...], 0.0)

    is_lat = i >= CTX_GROUPS
    seq_len = jnp.where(is_lat, DEC_SEQ, SEQ)
    pos0 = jnp.where(is_lat, (i & (LAT_GROUPS_PER_SEQ - 1)) * GROUP, 0)
    pos = lax.broadcasted_iota(jnp.int32, (GROUP, 1), 0) + pos0
    outs = []
    for gi, w in enumerate(POOL_WINDOWS):
        cols = slice(gi * POOL_GROUP, (gi + 1) * POOL_GROUP)
        tot = zp_ref[pl.ds(HALO - w // 2, GROUP), cols]
        for d in range(1, w):
            tot = tot + zp_ref[pl.ds(HALO - w // 2 + d, GROUP), cols]
        lo = jnp.clip(pos - w // 2, 0, seq_len)
        hi = jnp.clip(pos - w // 2 + w, 0, seq_len)
        mean = tot / (hi - lo).astype(F32)
        d = (mean - cur_ref[:, cols]).astype(BF16)
        outs.append((_dot(d, pw_ref[gi]) * ps_ref[:, cols]).astype(BF16))
    return jnp.concatenate(outs, axis=1)


def _merge_kernel(x_ref, a_ref, cc_ref, cp_ref, cn_ref, pc_ref, pp_ref, pn_ref,
                  g0_ref, g1_ref, g2_ref, g3_ref, gt_ref,
                  dw_ref, cb_ref, cg_ref, pw_ref, ps_ref,
                  wa_ref, wb_ref, wc_ref, wo_ref, o_ref,
                  hp_ref, acc_ref, sh_ref, zp_ref):
    i = pl.program_id(0)
    gate_refs = (g0_ref, g1_ref, g2_ref, g3_ref)
    conv = _conv_block(i, cc_ref, cp_ref, cn_ref, dw_ref, cb_ref, cg_ref, hp_ref, acc_ref, sh_ref)
    pool = _pool_block(i, pc_ref, pp_ref, pn_ref, pw_ref, ps_ref, zp_ref)

    def gate(k):
        parts = []
        lo, hi = k * D_MODEL, (k + 1) * D_MODEL
        while lo < hi:
            blk, start = divmod(lo, GATE_BLK)
            stop = min(GATE_BLK, start + hi - lo)
            parts.append(gate_refs[blk][:, start:stop])
            lo += stop - start
        return jax.nn.sigmoid(jnp.concatenate(parts, axis=1))

    merged = gate(0) * _dot(a_ref[...], wa_ref[...])
    merged += gate(1) * _dot(conv, wb_ref[...])
    merged += gate(2) * _dot(pool, wc_ref[...])
    o_ref[...] = x_ref[...] + gt_ref[...] * _dot(merged.astype(BF16), wo_ref[...])


def _merge(x, attn, u, gt, dw, cb, cg, pw, ps, wa, wb, wc, wo, l):
    assert MERGE_TILE == GROUP
    tok = lambda width, col=0: pl.BlockSpec((MERGE_TILE, width), lambda i: (i, col))
    full = lambda *shape: pl.BlockSpec(shape, lambda i: (0,) * len(shape))
    gate0 = COL_GATE // GATE_BLK
    weight = lambda rows: pl.BlockSpec((None, rows, D_MODEL), lambda i: (l, 0, 0),
                                       pipeline_mode=pl.Buffered(1))
    halo_rows = GROUP + 2 * HALO
    return pl.pallas_call(
        _merge_kernel,
        grid=(N_TOK // MERGE_TILE,),
        in_specs=[tok(D_MODEL), tok(D_ATTN)]
        + _halo_specs(2 * D_CONV, COL_CONV // (2 * D_CONV))
        + _halo_specs(D_POOL, COL_POOL // D_POOL)
        + [tok(GATE_BLK, gate0), tok(GATE_BLK, gate0 + 1), tok(GATE_BLK, gate0 + 2), tok(GATE_BLK, gate0 + 3),
           pl.BlockSpec((None, 1, D_MODEL), lambda i: (i, 0, 0)),
           full(CONV_WIDTH, D_CONV), full(1, D_CONV), full(1, D_CONV),
           full(len(POOL_WINDOWS), POOL_GROUP, POOL_GROUP), full(1, D_POOL),
           weight(D_ATTN), weight(D_CONV), weight(D_POOL), weight(D_MODEL)],
        out_specs=tok(D_MODEL),
        out_shape=jax.ShapeDtypeStruct((N_TOK, D_MODEL), F32),
        scratch_shapes=[pltpu.VMEM((halo_rows, D_CONV), F32), pltpu.VMEM((GROUP, D_CONV), F32),
                        pltpu.VMEM((SUBLANES, halo_rows, LANES), F32), pltpu.VMEM((halo_rows, D_POOL), F32)],
        compiler_params=_params(("arbitrary",), BIG_VMEM_LIMIT),
        name="merge",
    )(x, attn, u, u, u, u, u, u, u, u, u, u, gt, dw, cb, cg, pw, ps, wa, wb, wc, wo)


def kernel(x_prompt, x_sample, cache_k, cache_v, c, c_ctx, w_mod, b_mod, norm_g, ffn_w1, ffn_w3, ffn_w2,
           w_in, qk_g, rpb, w_br_a, conv_dw, conv_b, conv_g, w_br_b, pool_w, pool_scale, w_br_c, w_out):
    x = jnp.concatenate([x_prompt.reshape(N_CTX, D_MODEL), x_sample.reshape(N_LAT, D_MODEL)], axis=0)

    cpad = jnp.concatenate([c_ctx[None, :], c, jnp.zeros((8 - 1 - DEC_BATCH, D_MODEL), F32)], axis=0)
    mod = _modulation(cpad, w_mod, b_mod)
    group_row = jnp.concatenate([jnp.zeros((CTX_GROUPS,), jnp.int32),
                                 1 + jnp.arange(N_LAT // GROUP, dtype=jnp.int32) // LAT_GROUPS_PER_SEQ])
    mod = mod[:, group_row].reshape(DEPTH, N_GROUPS, N_MOD, 1, D_MODEL).transpose(0, 2, 1, 3, 4)

    w1_b, w3_b, w2_b, w_in_b = (w.astype(BF16) for w in (ffn_w1, ffn_w3, ffn_w2, w_in))
    tail0 = N_FF_TILES * FF_TILE
    w13t = jnp.concatenate([w1_b[..., tail0:], w3_b[..., tail0:]], axis=-1)
    w2t = w2_b[:, :, tail0:, :]
    wa_b, wb_b, wc_b, wo_b = (w.astype(BF16) for w in (w_br_a, w_br_b, w_br_c, w_out))
    pool_w_b = pool_w.astype(BF16)
    qk_pair = jnp.tile(qk_g, (1, 1, 2)).reshape(DEPTH, 2, 1, LANES)
    ck = cache_k.reshape(DEC_BATCH, DEPTH, PAST_LEN, D_ATTN)
    cv = cache_v.reshape(DEC_BATCH, DEPTH, PAST_LEN, D_ATTN)

    new_k = new_v = None

    for l in range(DEPTH):
        m = mod[l]
        ng = norm_g[l][:, None, :]
        ffn_w = (w1_b, w3_b, w2_b, w13t, w2t)
        x = _ffn(x, ng[0], m[0], m[1], m[2], *ffn_w, l, 0)
        u = _inproj(x, ng[1], m[3], m[4], w_in_b, l)
        attn, new_k, new_v = _ctx_attention(u, qk_pair[l, 0], qk_pair[l, 1], new_k, new_v, l)
        attn = _lat_attention(u, ck, cv, _bias_table(rpb[l]), qk_pair[l, 0], qk_pair[l, 1], attn, l)
        x = _merge(x, attn, u, m[5], conv_dw[l], conv_b[l][None, :], conv_g[l][None, :],
                   pool_w_b[l], pool_scale[l][None, :], wa_b, wb_b, wc_b, wo_b, l)
        x = _ffn(x, ng[2], m[6], m[7], m[8], *ffn_w, l, 1)

    y_prompt = x[:N_CTX].reshape(BATCH, SEQ, D_MODEL)
    y_sample = x[N_CTX:].reshape(DEC_BATCH, DEC_SEQ, D_MODEL)
    shape = (BATCH, DEPTH, SEQ, N_HEADS, HEAD_DIM)
    return y_prompt, y_sample, new_k.reshape(shape), new_v.reshape(shape)
```

```python
import functools

import jax
import jax.numpy as jnp
from jax import lax
from jax.experimental import pallas as pl
from jax.experimental.pallas import tpu as pltpu

D_MODEL = 2048
BATCH = 32
SEQ = 256
DEPTH = 4
DEC_BATCH = 4
DEC_SEQ = 1024
PAST_LEN = 512
GRID_W = 64
GRID_ROWS = DEC_SEQ // GRID_W
D_ATTN = D_MODEL // 2
N_HEADS = 16
HEAD_DIM = D_ATTN // N_HEADS
WIN_H = 8
WIN_W = 16
D_CONV = D_MODEL // 4
CONV_WIDTH = 31
D_POOL = D_MODEL // 4
POOL_WINDOWS = (2, 4, 8, 16)
POOL_GROUP = D_POOL // len(POOL_WINDOWS)
D_FF = ((8 * D_MODEL // 3 + 127) // 128) * 128
N_MOD = 9
EPS = 1e-6
IN_WIDTH = 3 * D_ATTN + 2 * D_CONV + D_POOL + 3 * D_MODEL

N_CTX = BATCH * SEQ
N_LAT = DEC_BATCH * DEC_SEQ
N_TOK = N_CTX + N_LAT
GROUP = 256
N_GROUPS = N_TOK // GROUP
CTX_GROUPS = N_CTX // GROUP
LAT_GROUPS_PER_SEQ = DEC_SEQ // GROUP

LANES = 128
SUBLANES = 8
HEAD_PAIRS = N_HEADS // 2
HALO = 16

COL_Q = 0
COL_K = D_ATTN
COL_V = 2 * D_ATTN
COL_CONV = 3 * D_ATTN
COL_POOL = COL_CONV + 2 * D_CONV
COL_GATE = COL_POOL + D_POOL

FF_TILE = 768
N_FF_STEPS = D_FF // FF_TILE
FF_LAST = D_FF - (N_FF_STEPS - 1) * FF_TILE
TOK_TILE = 512
IN_TOK_TILE = 1024
IN_TILE = 1536
MERGE_TILE = 256
GATE_BLK = 1536
MOD_TILE = 1024

Q_ROWS = 4
Q_BLK = Q_ROWS * GRID_W
N_QBLK = GRID_ROWS // Q_ROWS
KEY_ROWS = 12
KEY_BLK = KEY_ROWS * GRID_W
NEG = -1e30

VMEM_LIMIT = 48 * 1024 * 1024
BIG_VMEM_LIMIT = 56 * 1024 * 1024

F32 = jnp.float32
BF16 = jnp.bfloat16


def _params(sem, vmem=VMEM_LIMIT):
    return pltpu.CompilerParams(dimension_semantics=sem, vmem_limit_bytes=vmem)


def _dot(a, b):
    return jnp.dot(a, b, preferred_element_type=F32)


def _dot_nt(a, b):
    return lax.dot_general(a, b, (((1,), (1,)), ((), ())), preferred_element_type=F32)


def _adaln(x, g, sc, sh):
    ms = jnp.mean(x * x, axis=-1, keepdims=True)
    y = (x * lax.rsqrt(ms + EPS)) * g
    return y * (1.0 + sc) + sh


def _mod_kernel(c_ref, w_ref, b_ref, o_ref):
    c = c_ref[...]
    s = (c * jax.nn.sigmoid(c)).astype(BF16)
    o_ref[...] = _dot(s, w_ref[...].astype(BF16)) + b_ref[...]


def _modulation(cpad, w_mod, b_mod):
    width = N_MOD * D_MODEL
    return pl.pallas_call(
        _mod_kernel,
        grid=(DEPTH, width // MOD_TILE),
        in_specs=[
            pl.BlockSpec((8, D_MODEL), lambda l, n: (0, 0)),
            pl.BlockSpec((None, D_MODEL, MOD_TILE), lambda l, n: (l, 0, n)),
            pl.BlockSpec((None, 1, MOD_TILE), lambda l, n: (l, 0, n)),
        ],
        out_specs=pl.BlockSpec((None, 8, MOD_TILE), lambda l, n: (l, 0, n)),
        out_shape=jax.ShapeDtypeStruct((DEPTH, 8, width), F32),
        compiler_params=_params(("arbitrary", "arbitrary")),
        name="modulation",
    )(cpad, w_mod, b_mod.reshape(DEPTH, 1, width))


def _ffn_kernel(x_ref, ng_ref, sh_ref, sc_ref, gt_ref, w1_ref, w3_ref, w2_ref,
                w13l_ref, w2l_ref, o_ref, h_ref):
    j = pl.program_id(1)

    @pl.when(j == 0)
    def _():
        x = x_ref[...]
        h_ref[...] = _adaln(x, ng_ref[...], sc_ref[...], sh_ref[...]).astype(BF16)
        o_ref[...] = x

    def down(a, b, w2):
        g = (a * jax.nn.sigmoid(a) * b).astype(BF16)
        o_ref[...] += (0.5 * gt_ref[...]) * _dot(g, w2)

    @pl.when(j < N_FF_STEPS - 1)
    def _():
        h = h_ref[...]
        down(_dot(h, w1_ref[...]), _dot(h, w3_ref[...]), w2_ref[...])

    @pl.when(j == N_FF_STEPS - 1)
    def _():
        ab = _dot(h_ref[...], w13l_ref[...])
        down(ab[:, :FF_LAST], ab[:, FF_LAST:], w2l_ref[...])


def _ffn(x, ng, sh, sc, gt, w1, w3, w2, w13l, w2l, l, s):
    step = TOK_TILE // GROUP
    mod_spec = pl.BlockSpec((None, 1, D_MODEL), lambda i, j: (i * step, 0, 0))
    tile = lambda j: jnp.minimum(j, N_FF_STEPS - 2)
    return pl.pallas_call(
        _ffn_kernel,
        grid=(N_TOK // TOK_TILE, N_FF_STEPS),
        in_specs=[
            pl.BlockSpec((TOK_TILE, D_MODEL), lambda i, j: (i, 0)),
            pl.BlockSpec((1, D_MODEL), lambda i, j: (0, 0)),
            mod_spec, mod_spec, mod_spec,
            pl.BlockSpec((None, None, D_MODEL, FF_TILE), lambda i, j: (l, s, 0, tile(j))),
            pl.BlockSpec((None, None, D_MODEL, FF_TILE), lambda i, j: (l, s, 0, tile(j))),
            pl.BlockSpec((None, None, FF_TILE, D_MODEL), lambda i, j: (l, s, tile(j), 0)),
            pl.BlockSpec((None, None, D_MODEL, 2 * FF_LAST), lambda i, j: (l, s, 0, 0),
                         pipeline_mode=pl.Buffered(1)),
            pl.BlockSpec((None, None, FF_LAST, D_MODEL), lambda i, j: (l, s, 0, 0),
                         pipeline_mode=pl.Buffered(1)),
        ],
        out_specs=pl.BlockSpec((TOK_TILE, D_MODEL), lambda i, j: (i, 0)),
        out_shape=jax.ShapeDtypeStruct((N_TOK, D_MODEL), F32),
        scratch_shapes=[pltpu.VMEM((TOK_TILE, D_MODEL), BF16)],
        compiler_params=_params(("arbitrary", "arbitrary"), BIG_VMEM_LIMIT),
        name="ffn",
    )(x, ng, sh, sc, gt, w1, w3, w2, w13l, w2l)


def _inproj_kernel(x_ref, ng_ref, sh_ref, sc_ref, w_ref, o_ref, h_ref):
    @pl.when(pl.program_id(1) == 0)
    def _():
        h_ref[...] = _adaln(x_ref[...], ng_ref[...], sc_ref[...], sh_ref[...]).astype(BF16)

    o_ref[...] = _dot(h_ref[...], w_ref[...])


def _inproj(x, ng, sh, sc, w_in, l):
    step = IN_TOK_TILE // GROUP
    mod_spec = pl.BlockSpec((None, 1, D_MODEL), lambda i, j: (i * step, 0, 0))
    return pl.pallas_call(
        _inproj_kernel,
        grid=(N_TOK // IN_TOK_TILE, IN_WIDTH // IN_TILE),
        in_specs=[
            pl.BlockSpec((IN_TOK_TILE, D_MODEL), lambda i, j: (i, 0)),
            pl.BlockSpec((1, D_MODEL), lambda i, j: (0, 0)),
            mod_spec, mod_spec,
            pl.BlockSpec((None, D_MODEL, IN_TILE), lambda i, j: (l, 0, j)),
        ],
        out_specs=pl.BlockSpec((IN_TOK_TILE, IN_TILE), lambda i, j: (i, j)),
        out_shape=jax.ShapeDtypeStruct((N_TOK, IN_WIDTH), F32),
        scratch_shapes=[pltpu.VMEM((IN_TOK_TILE, D_MODEL), BF16)],
        compiler_params=_params(("arbitrary", "arbitrary"), BIG_VMEM_LIMIT),
        name="inproj",
    )(x, ng, sh, sc, w_in)


def _pair_rmsnorm(x, g, first):
    x2 = x * x
    s0 = jnp.sum(jnp.where(first, x2, 0.0), axis=-1, keepdims=True)
    s1 = jnp.sum(jnp.where(first, 0.0, x2), axis=-1, keepdims=True)
    inv = 1.0 / HEAD_DIM
    r = jnp.where(first, lax.rsqrt(s0 * inv + EPS), lax.rsqrt(s1 * inv + EPS))
    return (x * r) * g


def _first_head_mask():
    return lax.broadcasted_iota(jnp.int32, (1, LANES), 1) < HEAD_DIM


def _ctx_attn_kernel(first_layer, q_ref, k_ref, v_ref, qg_ref, kg_ref, *refs):
    o_ref, nk_ref, nv_ref = refs[-3:]
    if first_layer:
        nk_ref[1:] = jnp.zeros((DEPTH - 1, SEQ, D_ATTN), F32)
        nv_ref[1:] = jnp.zeros((DEPTH - 1, SEQ, D_ATTN), F32)
        nk_ref, nv_ref = nk_ref.at[0], nv_ref.at[0]
    first = _first_head_mask()
    for p in range(HEAD_PAIRS):
        cols = slice(p * LANES, (p + 1) * LANES)
        qn = _pair_rmsnorm(q_ref[:, cols], qg_ref[...], first) * (HEAD_DIM ** -0.5)
        kn = _pair_rmsnorm(k_ref[:, cols], kg_ref[...], first)
        v = v_ref[:, cols]
        nk_ref[:, cols] = kn
        nv_ref[:, cols] = v
        kb = kn.astype(BF16)
        vb = v.astype(BF16)
        outs = []
        for hh in range(2):
            sel = first if hh == 0 else jnp.logical_not(first)
            qm = jnp.where(sel, qn, 0.0).astype(BF16)
            s = _dot_nt(qm, kb)
            m = jnp.max(s, axis=-1, keepdims=True)
            e = jnp.exp(s - m)
            den = jnp.sum(e, axis=-1, keepdims=True)
            outs.append(_dot(e.astype(BF16), vb) / den)
        o_ref[:, cols] = jnp.where(first, outs[0], outs[1]).astype(BF16)


def _ctx_attention(u, qg, kg, new_k, new_v, l):
    qkv = lambda c: pl.BlockSpec((SEQ, D_ATTN), lambda b: (b, c))
    gain_spec = pl.BlockSpec((1, LANES), lambda b: (0, 0))
    any_spec = pl.BlockSpec(memory_space=pl.ANY)
    caches = () if new_k is None else (new_k, new_v)
    if caches:
        cache_spec = pl.BlockSpec((None, None, SEQ, D_ATTN), lambda b: (b, l, 0, 0))
    else:
        cache_spec = pl.BlockSpec((None, DEPTH, SEQ, D_ATTN), lambda b: (b, 0, 0, 0))
    cache_shape = jax.ShapeDtypeStruct((BATCH, DEPTH, SEQ, D_ATTN), F32)
    return pl.pallas_call(
        functools.partial(_ctx_attn_kernel, not caches),
        grid=(BATCH,),
        in_specs=[qkv(COL_Q // D_ATTN), qkv(COL_K // D_ATTN), qkv(COL_V // D_ATTN),
                  gain_spec, gain_spec] + [any_spec] * len(caches),
        out_specs=[pl.BlockSpec((SEQ, D_ATTN), lambda b: (b, 0)), cache_spec, cache_spec],
        out_shape=[jax.ShapeDtypeStruct((N_CTX, D_ATTN), BF16), cache_shape, cache_shape],
        input_output_aliases={5: 1, 6: 2} if caches else {},
        compiler_params=_params(("arbitrary",)),
        name="ctx_attention",
    )(u, u, u, qg, kg, *caches)


def _window_start(r):
    return min(max(r - WIN_H // 2, 0), GRID_ROWS - WIN_H)


def _key_window_row(blk):
    return min(max(_window_start(blk * Q_ROWS), 0), GRID_ROWS - KEY_ROWS)


def _bias_kernel(rpb_ref, o_ref, t_ref):
    h = pl.program_id(0)
    n_dc = 2 * WIN_W - 1
    n_dr = 2 * WIN_H - 1
    lane = lax.broadcasted_iota(jnp.int32, (GRID_W, LANES), 1)
    qc = lax.broadcasted_iota(jnp.int32, (GRID_W, LANES), 0)
    kc = lane & (GRID_W - 1)
    dc = kc - qc
    col_start = jnp.clip(qc - WIN_W // 2, 0, GRID_W - WIN_W)
    col_in = (kc >= col_start) & (kc < col_start + WIN_W)
    base = h * (n_dr * n_dc)
    for dr in range(n_dr):
        row = base + dr * n_dc
        t = jnp.full((GRID_W, LANES), rpb_ref[row], F32)
        for j in range(1, n_dc - 1):
            t = jnp.where(dc == j - (WIN_W - 1), rpb_ref[row + j], t)
        t = jnp.where(dc >= WIN_W - 1, rpb_ref[row + n_dc - 1], t)
        t_ref[dr] = jnp.where(col_in, t, NEG)

    left = lane < GRID_W
    neg = jnp.full((GRID_W, LANES), NEG, F32)
    for blk in range(N_QBLK):
        ws = _key_window_row(blk)
        for qi in range(Q_ROWS):
            qr = blk * Q_ROWS + qi
            st = _window_start(qr)

            def tile(kj):
                kr = ws + kj
                if st <= kr < st + WIN_H:
                    return t_ref[kr - qr + WIN_H - 1]
                return None

            for g in range(KEY_ROWS // 2):
                a, b = tile(2 * g), tile(2 * g + 1)
                if a is None and b is None:
                    val = neg
                else:
                    val = jnp.where(left, neg if a is None else a, neg if b is None else b)
                o_ref[blk, qi * GRID_W:(qi + 1) * GRID_W, g * LANES:(g + 1) * LANES] = val


def _bias_table(rpb_l):
    n = (2 * WIN_H - 1) * (2 * WIN_W - 1)
    return pl.pallas_call(
        _bias_kernel,
        grid=(N_HEADS,),
        in_specs=[pl.BlockSpec(memory_space=pltpu.SMEM)],
        out_specs=pl.BlockSpec((None, N_QBLK, Q_BLK, KEY_BLK), lambda h: (h, 0, 0, 0)),
        out_shape=jax.ShapeDtypeStruct((N_HEADS, N_QBLK, Q_BLK, KEY_BLK), F32),
        scratch_shapes=[pltpu.VMEM((2 * WIN_H - 1, GRID_W, LANES), F32)],
        compiler_params=_params(("arbitrary",)),
        name="bias_table",
    )(rpb_l.reshape(N_HEADS * n))


def _lat_attn_kernel(q_ref, k_ref, v_ref, ck_ref, cv_ref, bias_ref, qg_ref, kg_ref, o_ref):
    first = _first_head_mask()
    second = jnp.logical_not(first)
    qn = _pair_rmsnorm(q_ref[...], qg_ref[...], first) * (HEAD_DIM ** -0.5)
    kb = _pair_rmsnorm(k_ref[...], kg_ref[...], first).astype(BF16)
    vb = v_ref[...].astype(BF16)
    ckb = ck_ref[...].astype(BF16)
    cvb = cv_ref[...].astype(BF16)
    for blk in range(N_QBLK):
        rows = slice(blk * Q_BLK, (blk + 1) * Q_BLK)
        ws = _key_window_row(blk) * GRID_W
        kw = kb[ws:ws + KEY_BLK]
        vw = vb[ws:ws + KEY_BLK]
        outs = []
        for hh, sel in enumerate((first, second)):
            qm = jnp.where(sel, qn[rows], 0.0).astype(BF16)
            s_loc = _dot_nt(qm, kw) + bias_ref[hh, blk]
            s_ctx = _dot_nt(qm, ckb)
            m = jnp.maximum(jnp.max(s_loc, axis=-1, keepdims=True), jnp.max(s_ctx, axis=-1, keepdims=True))
            e_loc = jnp.exp(s_loc - m)
            e_ctx = jnp.exp(s_ctx - m)
            den = jnp.sum(e_loc, axis=-1, keepdims=True) + jnp.sum(e_ctx, axis=-1, keepdims=True)
            acc = _dot(e_loc.astype(BF16), vw) + _dot(e_ctx.astype(BF16), cvb)
            outs.append(acc / den)
        o_ref[rows, :] = jnp.where(first, outs[0], outs[1]).astype(BF16)


def _lat_attention(u, cache_k, cache_v, bias, qg, kg, l):
    seq0 = N_CTX // DEC_SEQ
    gain_spec = pl.BlockSpec((1, LANES), lambda b, p: (0, 0))
    cache_spec = pl.BlockSpec((None, None, PAST_LEN, LANES), lambda b, p: (b, l, 0, p))
    qkv = lambda col: pl.BlockSpec((DEC_SEQ, LANES), lambda b, p: (seq0 + b, col // LANES + p))
    return pl.pallas_call(
        _lat_attn_kernel,
        grid=(DEC_BATCH, HEAD_PAIRS),
        in_specs=[
            qkv(COL_Q), qkv(COL_K), qkv(COL_V),
            cache_spec, cache_spec,
            pl.BlockSpec((2, N_QBLK, Q_BLK, KEY_BLK), lambda b, p: (p, 0, 0, 0)),
            gain_spec, gain_spec,
        ],
        out_specs=pl.BlockSpec((DEC_SEQ, LANES), lambda b, p: (b, p)),
        out_shape=jax.ShapeDtypeStruct((N_LAT, D_ATTN), BF16),
        compiler_params=_params(("arbitrary", "arbitrary")),
        name="lat_attention",
    )(u, u, u, cache_k, cache_v, bias, qg, kg)


def _halo_masks(i):
    is_lat = i >= CTX_GROUPS
    j = i & (LAT_GROUPS_PER_SEQ - 1)
    return is_lat & (j != 0), is_lat & (j != LAT_GROUPS_PER_SEQ - 1)


def _conv_block(i, cur_ref, prev_ref, next_ref, dw_ref, cb_ref, cg_ref, hp_ref, acc_ref, sh_ref):
    has_prev, has_next = _halo_masks(i)

    def glu(z):
        return z[:, :D_CONV] * jax.nn.sigmoid(z[:, D_CONV:])

    hp_ref[0:HALO] = jnp.where(has_prev, glu(prev_ref[...]), 0.0)
    hp_ref[HALO:HALO + GROUP] = glu(cur_ref[...])
    hp_ref[HALO + GROUP:] = jnp.where(has_next, glu(next_ref[...]), 0.0)

    rows = 64
    off = HALO - CONV_WIDTH // 2
    span = GROUP + 2 * HALO - SUBLANES
    for lc in range(D_CONV // LANES):
        cols = slice(lc * LANES, (lc + 1) * LANES)
        for r in range(SUBLANES):
            sh_ref[r, 0:span] = hp_ref[pl.ds(r, span), cols]
        accs = [jnp.zeros((rows, LANES), F32) for _ in range(GROUP // rows)]
        for j in range(CONV_WIDTH):
            r = (off + j) % SUBLANES
            w = dw_ref[j:j + 1, cols]
            for rc in range(GROUP // rows):
                base = rc * rows + (off + j) - r
                accs[rc] = accs[rc] + sh_ref[r, base:base + rows] * w
        for rc in range(GROUP // rows):
            acc_ref[rc * rows:(rc + 1) * rows, cols] = accs[rc] + cb_ref[:, cols]

    y = acc_ref[...]
    ms = jnp.mean(y * y, axis=-1, keepdims=True)
    y = (y * lax.rsqrt(ms + EPS)) * cg_ref[...]
    return (y * jax.nn.sigmoid(y)).astype(BF16)


def _halo_specs(width, col):
    blocks_per_group = GROUP // HALO
    last = N_TOK // HALO - 1
    cur = pl.BlockSpec((GROUP, width), lambda i: (i, col))
    prev = pl.BlockSpec((HALO, width), lambda i: (jnp.maximum(i * blocks_per_group - 1, 0), col))
    nxt = pl.BlockSpec((HALO, width), lambda i: (jnp.minimum((i + 1) * blocks_per_group, last), col))
    return [cur, prev, nxt]


def _pool_block(i, cur_ref, prev_ref, next_ref, pw_ref, ps_ref, zp_ref):
    has_prev, has_next = _halo_masks(i)
    zp_ref[0:HALO] = jnp.where(has_prev, prev_ref[...], 0.0)
    zp_ref[HALO:HALO + GROUP] = cur_ref[...]
    zp_ref[HALO + GROUP:] = jnp.where(has_next, next_ref[...], 0.0)

    is_lat = i >= CTX_GROUPS
    seq_len = jnp.where(is_lat, DEC_SEQ, SEQ)
    pos0 = jnp.where(is_lat, (i & (LAT_GROUPS_PER_SEQ - 1)) * GROUP, 0)
    pos = lax.broadcasted_iota(jnp.int32, (GROUP, 1), 0) + pos0
    outs = []
    for gi, w in enumerate(POOL_WINDOWS):
        cols = slice(gi * POOL_GROUP, (gi + 1) * POOL_GROUP)
        tot = zp_ref[pl.ds(HALO - w // 2, GROUP), cols]
        for d in range(1, w):
            tot = tot + zp_ref[pl.ds(HALO - w // 2 + d, GROUP), cols]
        lo = jnp.clip(pos - w // 2, 0, seq_len)
        hi = jnp.clip(pos - w // 2 + w, 0, seq_len)
        mean = tot / (hi - lo).astype(F32)
        d = (mean - cur_ref[:, cols]).astype(BF16)
        outs.append((_dot(d, pw_ref[gi]) * ps_ref[:, cols]).astype(BF16))
    return jnp.concatenate(outs, axis=1)


def _merge_kernel(x_ref, ac_ref, al_ref, cc_ref, cp_ref, cn_ref, pc_ref, pp_ref, pn_ref,
                  g0_ref, g1_ref, g2_ref, g3_ref, gt_ref,
                  dw_ref, cb_ref, cg_ref, pw_ref, ps_ref,
                  wa_ref, wb_ref, wc_ref, wo_ref, o_ref,
                  hp_ref, acc_ref, sh_ref, zp_ref):
    i = pl.program_id(0)
    gate_refs = (g0_ref, g1_ref, g2_ref, g3_ref)
    conv = _conv_block(i, cc_ref, cp_ref, cn_ref, dw_ref, cb_ref, cg_ref, hp_ref, acc_ref, sh_ref)
    pool = _pool_block(i, pc_ref, pp_ref, pn_ref, pw_ref, ps_ref, zp_ref)

    def gate(k):
        parts = []
        lo, hi = k * D_MODEL, (k + 1) * D_MODEL
        while lo < hi:
            blk, start = divmod(lo, GATE_BLK)
            stop = min(GATE_BLK, start + hi - lo)
            parts.append(gate_refs[blk][:, start:stop])
            lo += stop - start
        return jax.nn.sigmoid(jnp.concatenate(parts, axis=1))

    attn = jnp.where(i < CTX_GROUPS, ac_ref[...], al_ref[...])
    merged = gate(0) * _dot(attn, wa_ref[...])
    merged += gate(1) * _dot(conv, wb_ref[...])
    merged += gate(2) * _dot(pool, wc_ref[...])
    o_ref[...] = x_ref[...] + gt_ref[...] * _dot(merged.astype(BF16), wo_ref[...])


def _merge(x, attn_ctx, attn_lat, u, gt, dw, cb, cg, pw, ps, wa, wb, wc, wo, l):
    assert MERGE_TILE == GROUP
    tok = lambda width, col=0: pl.BlockSpec((MERGE_TILE, width), lambda i: (i, col))
    full = lambda *shape: pl.BlockSpec(shape, lambda i: (0,) * len(shape))
    gate0 = COL_GATE // GATE_BLK
    weight = lambda rows: pl.BlockSpec((None, rows, D_MODEL), lambda i: (l, 0, 0),
                                       pipeline_mode=pl.Buffered(1))
    halo_rows = GROUP + 2 * HALO
    return pl.pallas_call(
        _merge_kernel,
        grid=(N_TOK // MERGE_TILE,),
        in_specs=[tok(D_MODEL),
                  pl.BlockSpec((MERGE_TILE, D_ATTN), lambda i: (jnp.minimum(i, CTX_GROUPS - 1), 0)),
                  pl.BlockSpec((MERGE_TILE, D_ATTN), lambda i: (jnp.maximum(i - CTX_GROUPS, 0), 0))]
        + _halo_specs(2 * D_CONV, COL_CONV // (2 * D_CONV))
        + _halo_specs(D_POOL, COL_POOL // D_POOL)
        + [tok(GATE_BLK, gate0), tok(GATE_BLK, gate0 + 1), tok(GATE_BLK, gate0 + 2), tok(GATE_BLK, gate0 + 3),
           pl.BlockSpec((None, 1, D_MODEL), lambda i: (i, 0, 0)),
           full(CONV_WIDTH, D_CONV), full(1, D_CONV), full(1, D_CONV),
           full(len(POOL_WINDOWS), POOL_GROUP, POOL_GROUP), full(1, D_POOL),
           weight(D_ATTN), weight(D_CONV), weight(D_POOL), weight(D_MODEL)],
        out_specs=tok(D_MODEL),
        out_shape=jax.ShapeDtypeStruct((N_TOK, D_MODEL), F32),
        scratch_shapes=[pltpu.VMEM((halo_rows, D_CONV), F32), pltpu.VMEM((GROUP, D_CONV), F32),
                        pltpu.VMEM((SUBLANES, halo_rows, LANES), F32), pltpu.VMEM((halo_rows, D_POOL), F32)],
        compiler_params=_params(("arbitrary",), BIG_VMEM_LIMIT),
        name="merge",
    )(x, attn_ctx, attn_lat, u, u, u, u, u, u, u, u, u, u, gt, dw, cb, cg, pw, ps, wa, wb, wc, wo)


def kernel(x_prompt, x_sample, cache_k, cache_v, c, c_ctx, w_mod, b_mod, norm_g, ffn_w1, ffn_w3, ffn_w2,
           w_in, qk_g, rpb, w_br_a, conv_dw, conv_b, conv_g, w_br_b, pool_w, pool_scale, w_br_c, w_out):
    x = jnp.concatenate([x_prompt.reshape(N_CTX, D_MODEL), x_sample.reshape(N_LAT, D_MODEL)], axis=0)

    cpad = jnp.concatenate([c_ctx[None, :], c, jnp.zeros((8 - 1 - DEC_BATCH, D_MODEL), F32)], axis=0)
    mod = _modulation(cpad, w_mod, b_mod)
    group_row = jnp.concatenate([jnp.zeros((CTX_GROUPS,), jnp.int32),
                                 1 + jnp.arange(N_LAT // GROUP, dtype=jnp.int32) // LAT_GROUPS_PER_SEQ])
    mod = mod[:, group_row].reshape(DEPTH, N_GROUPS, N_MOD, 1, D_MODEL).transpose(0, 2, 1, 3, 4)

    w1_b, w3_b, w2_b, w_in_b = (w.astype(BF16) for w in (ffn_w1, ffn_w3, ffn_w2, w_in))
    last0 = D_FF - FF_LAST
    w13l = jnp.concatenate([w1_b[..., last0:], w3_b[..., last0:]], axis=-1)
    w2l = w2_b[:, :, last0:, :]
    wa_b, wb_b, wc_b, wo_b = (w.astype(BF16) for w in (w_br_a, w_br_b, w_br_c, w_out))
    pool_w_b = pool_w.astype(BF16)
    qk_pair = jnp.tile(qk_g, (1, 1, 2)).reshape(DEPTH, 2, 1, LANES)
    ck = cache_k.reshape(DEC_BATCH, DEPTH, PAST_LEN, D_ATTN)
    cv = cache_v.reshape(DEC_BATCH, DEPTH, PAST_LEN, D_ATTN)

    new_k = new_v = None

    for l in range(DEPTH):
        m = mod[l]
        ng = norm_g[l][:, None, :]
        ffn_w = (w1_b, w3_b, w2_b, w13l, w2l)
        x = _ffn(x, ng[0], m[0], m[1], m[2], *ffn_w, l, 0)
        u = _inproj(x, ng[1], m[3], m[4], w_in_b, l)
        attn_ctx, new_k, new_v = _ctx_attention(u, qk_pair[l, 0], qk_pair[l, 1], new_k, new_v, l)
        attn_lat = _lat_attention(u, ck, cv, _bias_table(rpb[l]), qk_pair[l, 0], qk_pair[l, 1], l)
        x = _merge(x, attn_ctx, attn_lat, u, m[5], conv_dw[l], conv_b[l][None, :], conv_g[l][None, :],
                   pool_w_b[l], pool_scale[l][None, :], wa_b, wb_b, wc_b, wo_b, l)
        x = _ffn(x, ng[2], m[6], m[7], m[8], *ffn_w, l, 1)

    y_prompt = x[:N_CTX].reshape(BATCH, SEQ, D_MODEL)
    y_sample = x[N_CTX:].reshape(DEC_BATCH, DEC_SEQ, D_MODEL)
    shape = (BATCH, DEPTH, SEQ, N_HEADS, HEAD_DIM)
    return y_prompt, y_sample, new_k.reshape(shape), new_v.reshape(shape)
```
